```python
import numpy as np
import jax
import jax.numpy as jnp
from jax import lax

D_MODEL = 2048
BATCH = 4
SEQ = 4096
DEPTH = 2

GRID_W = 64
CTX_LEN = 256
ADA_CHUNKS = 6
NORM_EPS = 1e-6
NEG_INF = -1e30

D_RNN = D_MODEL // 2
LRU_BLOCKS = 8
LRU_BW = D_RNN // LRU_BLOCKS
CONV_W = 4
LRU_C = 8.0

MLA_HEADS = 8
MLA_NOPE = 128
MLA_ROPE = 64
MLA_QK = MLA_NOPE + MLA_ROPE
MLA_V = 128
Q_RANK = 512
KV_RANK = 512
ROPE_THETA = 10000.0
ATTN_BLOCK = 128
AB_IN = 2 * D_RNN + Q_RANK + KV_RANK + MLA_ROPE
AB_OUT = D_RNN + MLA_HEADS * MLA_V

NA_HEAD_DIM = 128
NA_HEADS = D_MODEL // NA_HEAD_DIM
NA_ROWS = 8
NA_COLS = 16
Q_COL_BLOCK = 16
KEY_COL_BLOCK = 32

N_EXPERTS = 16
N_GROUPS = 4
EXPERTS_PER_GROUP = N_EXPERTS // N_GROUPS
TOP_K = 2
D_EXPERT = 1408
D_SHARED = 1408

N_AB = (DEPTH + 1) // 2
N_NA = DEPTH // 2

kernel_name = "hybrid_rglru_mla_natten_moe_trunk"


def rms_norm(x, g):
    x32 = x.astype(jnp.float32)
    y = x32 * lax.rsqrt(jnp.mean(x32 * x32, axis=-1, keepdims=True) + NORM_EPS)
    return (y * g.astype(jnp.float32)).astype(x.dtype)


def modulate(x, g, shift, scale):
    return rms_norm(x, g) * (1 + scale) + shift


def rotate_axis(x, ang):
    cos = jnp.cos(ang)[None, :, None, :].astype(x.dtype)
    sin = jnp.sin(ang)[None, :, None, :].astype(x.dtype)
    x1, x2 = jnp.split(x, 2, axis=-1)
    return jnp.concatenate([x1 * cos - x2 * sin, x1 * sin + x2 * cos], axis=-1)


def rope_2d(x, n_rot, pos_row, pos_col):
    half = n_rot // 2
    inv_freq = ROPE_THETA ** (-jnp.arange(0, half, 2, dtype=jnp.float32) / half)
    ang_row = pos_row.astype(jnp.float32)[:, None] * inv_freq
    ang_col = pos_col.astype(jnp.float32)[:, None] * inv_freq
    x_pass, x_row, x_col = x[..., :-n_rot], x[..., -n_rot:-half], x[..., -half:]
    return jnp.concatenate([x_pass, rotate_axis(x_row, ang_row), rotate_axis(x_col, ang_col)], axis=-1)


def dense_attend(q, k, v, scale):
    s = jnp.einsum('bqhd,bkhd->bhqk', q, k).astype(jnp.float32) * scale
    p = jax.nn.softmax(s, axis=-1).astype(v.dtype)
    return jnp.einsum('bhqk,bkhd->bqhd', p, v)


def blocked_attend(q, k, v, scale):
    B, L, H, dq = q.shape
    qb = q.reshape(B, L // ATTN_BLOCK, ATTN_BLOCK, H, dq).transpose(1, 0, 2, 3, 4)
    out = lax.map(lambda qi: dense_attend(qi, k, v, scale), qb)
    return out.transpose(1, 0, 2, 3, 4).reshape(B, L, H, v.shape[-1])


def centred_depthwise_conv(x, w, b):
    left = CONV_W // 2
    right = CONV_W - 1 - left
    T = x.shape[1]
    xp = jnp.pad(x, ((0, 0), (left, right), (0, 0)))
    return sum(xp[:, j:j + T] * w[j] for j in range(CONV_W)) + b


def lru_coeffs(x, w_a, b_a, w_i, b_i, lam):
    B, T, _ = x.shape
    xb = x.reshape(B, T, LRU_BLOCKS, LRU_BW)
    r = jax.nn.sigmoid((jnp.einsum('btnk,nkj->btnj', xb, w_a).reshape(B, T, D_RNN) + b_a).astype(jnp.float32))
    i = jax.nn.sigmoid((jnp.einsum('btnk,nkj->btnj', xb, w_i).reshape(B, T, D_RNN) + b_i).astype(jnp.float32))
    log_a = -LRU_C * r * jax.nn.softplus(-lam.astype(jnp.float32))
    a = jnp.exp(log_a)
    b = jnp.sqrt(-jnp.expm1(2 * log_a)) * i * x.astype(jnp.float32)
    return a, b


def linear_scan(a, b, h0, reverse):
    def combine(e1, e2):
        a1, b1 = e1
        a2, b2 = e2
        return a1 * a2, a2 * b1 + b2
    a_cum, h = lax.associative_scan(combine, (a, b), axis=1, reverse=reverse)
    if h0 is not None:
        h = h + a_cum * h0[:, None, :]
    return h


def ab_mixer(hc, h, w_in, w_out, conv_w, conv_b, w_a, b_a, w_i, b_i, lam,
             q_norm_g, w_uq, kv_norm_g, w_ukv, q_head_g, k_head_g, with_ctx_out):
    B, S, _ = hc.shape
    L = h.shape[1]
    T = S + L
    u = jnp.concatenate([hc, h], axis=1) @ w_in
    o1, o2 = D_RNN, 2 * D_RNN
    o3, o4 = o2 + Q_RANK, o2 + Q_RANK + KV_RANK
    u_x, u_gate, u_cq, u_ckv, u_kr = u[..., :o1], u[..., o1:o2], u[..., o2:o3], u[..., o3:o4], u[..., o4:]

    xconv = jnp.concatenate([centred_depthwise_conv(u_x[:, :S], conv_w, conv_b),
                             centred_depthwise_conv(u_x[:, S:], conv_w, conv_b)], axis=1)
    a_f, b_f = lru_coeffs(xconv, w_a[0], b_a[0], w_i[0], b_i[0], lam[0])
    a_b, b_b = lru_coeffs(xconv, w_a[1], b_a[1], w_i[1], b_i[1], lam[1])
    hc_f = linear_scan(a_f[:, :S], b_f[:, :S], None, False)
    hc_b = linear_scan(a_b[:, :S], b_b[:, :S], None, True)
    hl_f = linear_scan(a_f[:, S:], b_f[:, S:], hc_f[:, -1], False)
    hl_b = linear_scan(a_b[:, S:], b_b[:, S:], hc_b[:, 0], True)
    gate = jax.nn.gelu(u_gate)
    rnn_l = (hl_f + hl_b).astype(h.dtype) * gate[:, S:]

    q = (rms_norm(u_cq, q_norm_g) @ w_uq).reshape(B, T, MLA_HEADS, MLA_QK)
    kv = (rms_norm(u_ckv, kv_norm_g) @ w_ukv).reshape(B, T, MLA_HEADS, MLA_NOPE + MLA_V)
    k_nope, v = kv[..., :MLA_NOPE], kv[..., MLA_NOPE:]
    k_rope = jnp.broadcast_to(u_kr[:, :, None, :], (B, T, MLA_HEADS, MLA_ROPE))
    k = jnp.concatenate([k_nope, k_rope], axis=-1)
    q = rms_norm(q, q_head_g)
    k = rms_norm(k, k_head_g)
    t = jnp.arange(L)
    pos_row, pos_col = t // GRID_W, t % GRID_W
    q_l = rope_2d(q[:, S:], MLA_ROPE, pos_row, pos_col)
    k_l = rope_2d(k[:, S:], MLA_ROPE, pos_row, pos_col)
    k_all = jnp.concatenate([k[:, :S], k_l], axis=1)
    scale = MLA_QK ** -0.5
    att_l = blocked_attend(q_l, k_all, v, scale)
    y_l = jnp.concatenate([rnn_l, att_l.reshape(B, L, MLA_HEADS * MLA_V)], axis=-1) @ w_out
    if not with_ctx_out:
        return None, y_l
    rnn_c = (hc_f + hc_b).astype(hc.dtype) * gate[:, :S]
    att_c = dense_attend(q[:, :S], k[:, :S], v[:, :S], scale)
    y_c = jnp.concatenate([rnn_c, att_c.reshape(B, S, MLA_HEADS * MLA_V)], axis=-1) @ w_out
    return y_c, y_l


def neighbourhood_attend(q, k, v, k_ctx, v_ctx, rpb, scale):
    B, L, H, dh = q.shape
    rows = L // GRID_W
    kr = min(NA_ROWS, rows)
    n_cb = GRID_W // Q_COL_BLOCK
    cols = np.arange(GRID_W)
    q_cols = cols.reshape(n_cb, Q_COL_BLOCK)
    q_col_start = np.clip(cols - NA_COLS // 2, 0, GRID_W - NA_COLS).reshape(n_cb, Q_COL_BLOCK)
    blk_start = np.clip(np.arange(n_cb) * Q_COL_BLOCK - NA_COLS // 2, 0, GRID_W - KEY_COL_BLOCK)
    key_cols = blk_start[:, None] + np.arange(KEY_COL_BLOCK)[None, :]
    kc_b = key_cols[:, None, :]
    col_ok = (kc_b >= q_col_start[..., None]) & (kc_b < q_col_start[..., None] + NA_COLS)
    dcol_idx = np.clip(kc_b - q_cols[..., None] + NA_COLS - 1, 0, 2 * NA_COLS - 2)
    rpb_cols = rpb[:, :, dcol_idx]
    kg = k.reshape(B, rows, GRID_W, H, dh)
    vg = v.reshape(B, rows, GRID_W, H, dh)
    qg = q.reshape(B, rows, n_cb, Q_COL_BLOCK, H, dh).transpose(1, 0, 2, 3, 4, 5)
    n_loc = kr * KEY_COL_BLOCK

    def one_row(args):
        r, q_row = args
        r0 = jnp.clip(r - kr // 2, 0, rows - kr)
        k_blk = lax.dynamic_slice_in_dim(kg, r0, kr, axis=1)[:, :, key_cols]
        v_blk = lax.dynamic_slice_in_dim(vg, r0, kr, axis=1)[:, :, key_cols]
        drow_idx = r0 - r + jnp.arange(kr) + NA_ROWS - 1
        bias = rpb_cols[:, drow_idx].transpose(0, 2, 3, 1, 4)
        s_loc = jnp.einsum('bnqhd,bknjhd->bhnqkj', q_row, k_blk).astype(jnp.float32) * scale + bias.astype(jnp.float32)
        s_loc = jnp.where(col_ok[:, :, None, :], s_loc, NEG_INF)
        s_ctx = jnp.einsum('bnqhd,bshd->bhnqs', q_row, k_ctx).astype(jnp.float32) * scale
        s = jnp.concatenate([s_loc.reshape(B, H, n_cb, Q_COL_BLOCK, n_loc), s_ctx], axis=-1)
        p = jax.nn.softmax(s, axis=-1).astype(v.dtype)
        p_loc = p[..., :n_loc].reshape(B, H, n_cb, Q_COL_BLOCK, kr, KEY_COL_BLOCK)
        return (jnp.einsum('bhnqkj,bknjhd->bnqhd', p_loc, v_blk)
                + jnp.einsum('bhnqs,bshd->bnqhd', p[..., n_loc:], v_ctx))

    out = lax.map(one_row, (jnp.arange(rows), qg))
    return out.transpose(1, 0, 2, 3, 4, 5).reshape(B, L, H, dh)


def na_mixer(hc, h, w_qkv, w_out, q_head_g, k_head_g, rpb, with_ctx_out):
    B, S, _ = hc.shape
    L = h.shape[1]
    u = (jnp.concatenate([hc, h], axis=1) @ w_qkv).reshape(B, S + L, 3, NA_HEADS, NA_HEAD_DIM)
    q = rms_norm(u[:, :, 0], q_head_g)
    k = rms_norm(u[:, :, 1], k_head_g)
    v = u[:, :, 2]
    scale = NA_HEAD_DIM ** -0.5
    att_l = neighbourhood_attend(q[:, S:], k[:, S:], v[:, S:], k[:, :S], v[:, :S], rpb, scale)
    y_l = att_l.reshape(B, L, D_MODEL) @ w_out
    if not with_ctx_out:
        return None, y_l
    y_c = dense_attend(q[:, :S], k[:, :S], v[:, :S], scale).reshape(B, S, D_MODEL) @ w_out
    return y_c, y_l


def swiglu(h, w_gate, w_up, w_down):
    return (jax.nn.silu(h @ w_gate) * (h @ w_up)) @ w_down


def grouped_moe(h, router_w, router_b, w_gate, w_up, w_down, ws_gate, ws_up, ws_down):
    shape = h.shape
    hf = h.reshape(-1, D_MODEL)
    n = hf.shape[0]
    scores = jax.nn.sigmoid((hf @ router_w).astype(jnp.float32))
    sel = scores + router_b.astype(jnp.float32)
    group_score = lax.top_k(sel.reshape(n, N_GROUPS, EXPERTS_PER_GROUP), TOP_K)[0].sum(-1)
    best = jnp.argmax(group_score, axis=-1)
    in_group = (jnp.arange(N_EXPERTS)[None, :] // EXPERTS_PER_GROUP) == best[:, None]
    _, idx = lax.top_k(jnp.where(in_group, sel, NEG_INF), TOP_K)
    w = jnp.take_along_axis(scores, idx, axis=-1)
    w = w / jnp.sum(w, axis=-1, keepdims=True)
    combine = jnp.sum(jax.nn.one_hot(idx, N_EXPERTS, dtype=jnp.float32) * w[..., None], axis=1).astype(h.dtype)
    y = swiglu(hf, ws_gate, ws_up, ws_down)
    for e in range(N_EXPERTS):
        y = y + combine[:, e:e + 1] * swiglu(hf, w_gate[e], w_up[e], w_down[e])
    return y.reshape(shape)


def setup_inputs(seed: int = 0) -> dict:
    key = jax.random.key(seed)
    ks = iter(jax.random.split(key, 64))
    f32 = jnp.float32

    def nrm(shape, scale):
        return jax.random.normal(next(ks), shape, f32) * scale

    def gain(shape):
        return 1.0 + nrm(shape, 0.05)

    a8 = jax.random.uniform(next(ks), (N_AB, 2, D_RNN), f32, minval=0.9, maxval=0.999)
    a_base = a8 ** (1.0 / LRU_C)
    lam = jnp.log(a_base) - jnp.log1p(-a_base)
    return {
        "x": nrm((BATCH, SEQ, D_MODEL), 1.0),
        "c": nrm((BATCH, D_MODEL), 1.0),
        "ctx": nrm((BATCH, CTX_LEN, D_MODEL), 1.0),
        "c_ctx": nrm((D_MODEL,), 1.0),
        "ada_w": nrm((DEPTH, D_MODEL, ADA_CHUNKS * D_MODEL), 0.5 * D_MODEL ** -0.5),
        "ada_b": nrm((DEPTH, ADA_CHUNKS * D_MODEL), 0.02),
        "norm_mix_g": gain((DEPTH, D_MODEL)),
        "norm_ffn_g": gain((DEPTH, D_MODEL)),
        "ab_w_in": nrm((N_AB, D_MODEL, AB_IN), D_MODEL ** -0.5),
        "ab_w_out": nrm((N_AB, AB_OUT, D_MODEL), AB_OUT ** -0.5),
        "lru_conv_w": nrm((N_AB, CONV_W, D_RNN), CONV_W ** -0.5),
        "lru_conv_b": nrm((N_AB, D_RNN), 0.02),
        "lru_w_a": nrm((N_AB, 2, LRU_BLOCKS, LRU_BW, LRU_BW), LRU_BW ** -0.5),
        "lru_b_a": nrm((N_AB, 2, D_RNN), 0.1),
        "lru_w_i": nrm((N_AB, 2, LRU_BLOCKS, LRU_BW, LRU_BW), LRU_BW ** -0.5),
        "lru_b_i": nrm((N_AB, 2, D_RNN), 0.1),
        "lru_lambda": lam,
        "mla_q_norm_g": gain((N_AB, Q_RANK)),
        "mla_w_uq": nrm((N_AB, Q_RANK, MLA_HEADS * MLA_QK), Q_RANK ** -0.5),
        "mla_kv_norm_g": gain((N_AB, KV_RANK)),
        "mla_w_ukv": nrm((N_AB, KV_RANK, MLA_HEADS * (MLA_NOPE + MLA_V)), KV_RANK ** -0.5),
        "mla_q_head_g": gain((N_AB, MLA_QK)),
        "mla_k_head_g": gain((N_AB, MLA_QK)),
        "na_w_qkv": nrm((N_NA, D_MODEL, 3 * D_MODEL), D_MODEL ** -0.5),
        "na_w_out": nrm((N_NA, D_MODEL, D_MODEL), D_MODEL ** -0.5),
        "na_q_head_g": gain((N_NA, NA_HEAD_DIM)),
        "na_k_head_g": gain((N_NA, NA_HEAD_DIM)),
        "na_rpb": nrm((N_NA, NA_HEADS, 2 * NA_ROWS - 1, 2 * NA_COLS - 1), 0.5),
        "router_w": nrm((D_MODEL, N_EXPERTS), D_MODEL ** -0.5),
        "router_b": nrm((N_EXPERTS,), 0.01),
        "moe_w_gate": nrm((DEPTH, N_EXPERTS, D_MODEL, D_EXPERT), D_MODEL ** -0.5),
        "moe_w_up": nrm((DEPTH, N_EXPERTS, D_MODEL, D_EXPERT), D_MODEL ** -0.5),
        "moe_w_down": nrm((DEPTH, N_EXPERTS, D_EXPERT, D_MODEL), D_EXPERT ** -0.5),
        "moe_ws_gate": nrm((DEPTH, D_MODEL, D_SHARED), D_MODEL ** -0.5),
        "moe_ws_up": nrm((DEPTH, D_MODEL, D_SHARED), D_MODEL ** -0.5),
        "moe_ws_down": nrm((DEPTH, D_SHARED, D_MODEL), D_SHARED ** -0.5),
    }


def reference(x, c, ctx, c_ctx, ada_w, ada_b, norm_mix_g, norm_ffn_g,
              ab_w_in, ab_w_out, lru_conv_w, lru_conv_b, lru_w_a, lru_b_a, lru_w_i, lru_b_i, lru_lambda,
              mla_q_norm_g, mla_w_uq, mla_kv_norm_g, mla_w_ukv, mla_q_head_g, mla_k_head_g,
              na_w_qkv, na_w_out, na_q_head_g, na_k_head_g, na_rpb,
              router_w, router_b, moe_w_gate, moe_w_up, moe_w_down, moe_ws_gate, moe_ws_up, moe_ws_down):
    S = ctx.shape[1]
    xc = ctx
    silu_c = jax.nn.silu(c)
    silu_cc = jax.nn.silu(c_ctx)
    for i in range(DEPTH):
        last = i == DEPTH - 1
        j = i // 2
        mod_l = jnp.split((silu_c @ ada_w[i] + ada_b[i])[:, None, :], ADA_CHUNKS, axis=-1)
        mod_c = jnp.split(silu_cc @ ada_w[i] + ada_b[i], ADA_CHUNKS, axis=-1)
        h = modulate(x, norm_mix_g[i], mod_l[0], mod_l[1])
        hc = modulate(xc, norm_mix_g[i], mod_c[0], mod_c[1])
        if i % 2 == 0:
            y_c, y = ab_mixer(hc, h, ab_w_in[j], ab_w_out[j], lru_conv_w[j], lru_conv_b[j],
                              lru_w_a[j], lru_b_a[j], lru_w_i[j], lru_b_i[j], lru_lambda[j],
                              mla_q_norm_g[j], mla_w_uq[j], mla_kv_norm_g[j], mla_w_ukv[j],
                              mla_q_head_g[j], mla_k_head_g[j], not last)
        else:
            y_c, y = na_mixer(hc, h, na_w_qkv[j], na_w_out[j], na_q_head_g[j], na_k_head_g[j],
                              na_rpb[j], not last)
        x = x + mod_l[2] * y
        h2 = modulate(x, norm_ffn_g[i], mod_l[3], mod_l[4])
        if last:
            x = x + mod_l[5] * grouped_moe(h2, router_w, router_b, moe_w_gate[i], moe_w_up[i], moe_w_down[i],
                                           moe_ws_gate[i], moe_ws_up[i], moe_ws_down[i])
        else:
            xc = xc + mod_c[2] * y_c
            h2c = modulate(xc, norm_ffn_g[i], mod_c[3], mod_c[4])
            f = grouped_moe(jnp.concatenate([h2c, h2], axis=1), router_w, router_b, moe_w_gate[i], moe_w_up[i],
                            moe_w_down[i], moe_ws_gate[i], moe_ws_up[i], moe_ws_down[i])
            xc = xc + mod_c[5] * f[:, :S]
            x = x + mod_l[5] * f[:, S:]
    return x
```

```python
import functools

import numpy as np
import jax
import jax.numpy as jnp
from jax import lax
from jax.experimental import pallas as pl
from jax.experimental.pallas import tpu as pltpu

F32 = jnp.float32
BF16 = jnp.bfloat16
I32 = jnp.int32

D = 2048
ADA = 6
EPS = 1e-6
NEG = -1e30
GRID_W = 64
D_RNN = 1024
LRU_BLOCKS = 8
LRU_BW = 128
LRU_C = 8.0
MLA_H = 8
MLA_NOPE = 128
MLA_ROPE = 64
MLA_QK = 192
MLA_V = 128
Q_RANK = 512
KV_RANK = 512
ROPE_THETA = 10000.0
AB_IN = 3136
AB_IN_PAD = 3200
NA_H = 16
NA_DH = 128
NA_ROWS = 8
NA_COLS = 16
N_EXP = 16
N_GROUPS = 4
EXP_PER_GROUP = 4
D_EXP = 1408

LANE = 128
TM = 256
VMEM_LIMIT = 56 * 1024 * 1024

NA_QR = 4
NA_KR = 12


def _cparams(sem, vmem=VMEM_LIMIT):
    return pltpu.CompilerParams(dimension_semantics=sem, vmem_limit_bytes=vmem)


def _resident(shape, index_map):
    return pl.BlockSpec(shape, index_map, pipeline_mode=pl.Buffered(1))


def _rms(x, g, n):
    ms = jnp.sum(x * x, axis=-1, keepdims=True) * (1.0 / n)
    return (x * lax.rsqrt(ms + EPS)) * g


def _modulate(x, g, shift, scale):
    return _rms(x, g, D) * (1.0 + scale) + shift


def _gelu_tanh(x):
    c = 0.7978845608028654
    return x * (0.5 * (1.0 + jnp.tanh(c * (x + 0.044715 * (x * x * x)))))


def _silu(x):
    return x * jax.nn.sigmoid(x)


ADA_TN = 1536


def _adaln_kernel(cs_ref, w_ref, b_ref, o_ref):
    a = _silu(cs_ref[...]).astype(BF16)
    o_ref[...] = jnp.dot(a, w_ref[...].astype(BF16), preferred_element_type=F32) + b_ref[...]


def _adaln(cs, ada_w, ada_b):
    depth = ada_w.shape[0]
    n = ADA * D
    return pl.pallas_call(
        _adaln_kernel,
        grid=(depth, n // ADA_TN),
        in_specs=[
            pl.BlockSpec((8, D), lambda l, j: (0, 0)),
            pl.BlockSpec((None, D, ADA_TN), lambda l, j: (l, 0, j)),
            pl.BlockSpec((None, 1, ADA_TN), lambda l, j: (l, 0, j)),
        ],
        out_specs=pl.BlockSpec((None, 8, ADA_TN), lambda l, j: (l, 0, j)),
        out_shape=jax.ShapeDtypeStruct((depth, 8, n), F32),
        compiler_params=_cparams(("arbitrary", "arbitrary")),
        name="adaln",
    )(cs, ada_w, ada_b.reshape(depth, 1, n))


def _mod_spec(nb, off):
    return pl.BlockSpec((None, 1, ADA * D), lambda b, i: (jnp.where(i + off == 0, nb, b), 0, 0))


def _ab_in_kernel(x_ref, g_ref, mod_ref, win_ref, qng_ref, kvng_ref, wuq_ref, wukv_ref, hg_ref, cos_ref, sin_ref,
                  ux_ref, gate_ref, q_ref, k_ref, v_ref):
    h = _modulate(x_ref[...], g_ref[...], mod_ref[:, 0:D], mod_ref[:, D:2 * D]).astype(BF16)
    u = jnp.dot(h, win_ref[...], preferred_element_type=F32)
    ux_ref[...] = u[:, 0:D_RNN].astype(BF16)
    gate_ref[...] = _gelu_tanh(u[:, D_RNN:2 * D_RNN]).astype(BF16)
    o2 = 2 * D_RNN
    cq = _rms(u[:, o2:o2 + Q_RANK], qng_ref[...], Q_RANK).astype(BF16)
    ckv = _rms(u[:, o2 + Q_RANK:o2 + Q_RANK + KV_RANK], kvng_ref[...], KV_RANK).astype(BF16)
    qf = jnp.dot(cq, wuq_ref[...], preferred_element_type=F32)
    kvf = jnp.dot(ckv, wukv_ref[...], preferred_element_type=F32)
    kr = u[:, o2 + Q_RANK + KV_RANK:AB_IN_PAD]
    cos = cos_ref[...]
    sin = sin_ref[...]
    lane = lax.broadcasted_iota(I32, (TM, LANE), 1)
    first_half = (lane % 32) < 16

    def rope(z):
        partner = jnp.where(first_half, pltpu.roll(z, LANE - 16, 1), pltpu.roll(z, 16, 1))
        return z * cos + partner * sin

    hg = hg_ref[...]
    kr_ss = jnp.sum(kr * kr, axis=-1, keepdims=True)
    nh = MLA_H * MLA_NOPE
    for hh in range(MLA_H):
        qn = qf[:, hh * LANE:(hh + 1) * LANE]
        qr = qf[:, nh + hh * LANE:nh + (hh + 1) * LANE]
        ss = jnp.sum(qn * qn, axis=-1, keepdims=True) + jnp.sum(qr * qr, axis=-1, keepdims=True)
        inv = lax.rsqrt(ss * (1.0 / MLA_QK) + EPS)
        q_ref[:, hh * 256:hh * 256 + LANE] = ((qn * inv) * hg[0:1]).astype(BF16)
        q_ref[:, hh * 256 + LANE:(hh + 1) * 256] = rope((qr * inv) * hg[1:2]).astype(BF16)
        kn = kvf[:, hh * LANE:(hh + 1) * LANE]
        ss = jnp.sum(kn * kn, axis=-1, keepdims=True) + kr_ss
        inv = lax.rsqrt(ss * (1.0 / MLA_QK) + EPS)
        k_ref[:, hh * 256:hh * 256 + LANE] = ((kn * inv) * hg[2:3]).astype(BF16)
        k_ref[:, hh * 256 + LANE:(hh + 1) * 256] = rope((kr * inv) * hg[3:4]).astype(BF16)
    v_ref[...] = kvf[:, nh:].astype(BF16)


def _ab_in(x_all, g, mod, w):
    nb, t, _ = x_all.shape
    nt = t // TM
    row = lambda b, i: (b, i, 0)
    const = lambda b, i: (0, 0)
    outs = [
        jax.ShapeDtypeStruct((nb, t, D_RNN), BF16),
        jax.ShapeDtypeStruct((nb, t, D_RNN), BF16),
        jax.ShapeDtypeStruct((nb, t, MLA_H * 256), BF16),
        jax.ShapeDtypeStruct((nb, t, MLA_H * 256), BF16),
        jax.ShapeDtypeStruct((nb, t, MLA_H * MLA_V), BF16),
    ]
    return pl.pallas_call(
        _ab_in_kernel,
        grid=(nb, nt),
        in_specs=[
            pl.BlockSpec((None, TM, D), row),
            _resident((1, D), const),
            _mod_spec(nb, 0),
            _resident((D, AB_IN_PAD), const),
            _resident((1, Q_RANK), const),
            _resident((1, KV_RANK), const),
            _resident((Q_RANK, 2 * MLA_H * LANE), const),
            _resident((KV_RANK, 2 * MLA_H * LANE), const),
            _resident((4, LANE), const),
            pl.BlockSpec((TM, LANE), lambda b, i: (i, 0)),
            pl.BlockSpec((TM, LANE), lambda b, i: (i, 0)),
        ],
        out_specs=[
            pl.BlockSpec((None, TM, D_RNN), row),
            pl.BlockSpec((None, TM, D_RNN), row),
            pl.BlockSpec((None, TM, MLA_H * 256), row),
            pl.BlockSpec((None, TM, MLA_H * 256), row),
            pl.BlockSpec((None, TM, MLA_H * MLA_V), row),
        ],
        out_shape=outs,
        compiler_params=_cparams(("arbitrary", "arbitrary")),
        name="ab_in",
    )(x_all, g, mod, w["w_in"], w["qng"], w["kvng"], w["w_uq"], w["w_ukv"], w["hg"], w["cos"], w["sin"])


def _lru_dir(prev_ref, cur_ref, next_ref, c, n_chunks, d, reverse,
             cw_ref, cb_ref, wa_ref, ba_ref, wi_ref, bi_ref, lam_ref, carry_ref, out_ref):
    xc = cur_ref[...].astype(F32)
    row = lax.broadcasted_iota(I32, (TM, 1), 0)
    left_ok = jnp.where(c >= 2, 1.0, 0.0)
    right_ok = jnp.where(jnp.logical_and(c >= 1, c <= n_chunks - 2), 1.0, 0.0)
    p0 = prev_ref[TM - 2:TM - 1, :].astype(F32) * left_ok
    p1 = prev_ref[TM - 1:TM, :].astype(F32) * left_ok
    n0 = next_ref[0:1, :].astype(F32) * right_ok
    x_m2 = jnp.where(row >= 2, pltpu.roll(xc, 2, 0), jnp.where(row == 0, p0, p1))
    x_m1 = jnp.where(row >= 1, pltpu.roll(xc, 1, 0), p1)
    x_p1 = jnp.where(row <= TM - 2, pltpu.roll(xc, TM - 1, 0), n0)
    cw = cw_ref[...]
    xconv = x_m2 * cw[0:1] + x_m1 * cw[1:2] + xc * cw[2:3] + x_p1 * cw[3:4] + cb_ref[...]

    xb = xconv.astype(BF16)
    ra = []
    ri = []
    for n in range(LRU_BLOCKS):
        xs = xb[:, n * LRU_BW:(n + 1) * LRU_BW]
        ra.append(jnp.dot(xs, wa_ref[d, n], preferred_element_type=F32))
        ri.append(jnp.dot(xs, wi_ref[d, n], preferred_element_type=F32))
    r = jax.nn.sigmoid(jnp.concatenate(ra, axis=1) + ba_ref[d])
    gi = jax.nn.sigmoid(jnp.concatenate(ri, axis=1) + bi_ref[d])
    neg_lam = -lam_ref[d]
    softplus = jnp.maximum(neg_lam, 0.0) + jnp.log1p(jnp.exp(-jnp.abs(neg_lam)))
    log_a = (-LRU_C * r) * softplus
    a = jnp.exp(log_a)
    th = jnp.tanh(log_a)
    neg_expm1 = (-2.0 * th) / (1.0 - th)
    bb = jnp.sqrt(neg_expm1) * gi * xconv

    k = 1
    while k < TM:
        if reverse:
            keep = row < TM - k
            a_s = jnp.where(keep, pltpu.roll(a, TM - k, 0), 1.0)
            b_s = jnp.where(keep, pltpu.roll(bb, TM - k, 0), 0.0)
        else:
            keep = row >= k
            a_s = jnp.where(keep, pltpu.roll(a, k, 0), 1.0)
            b_s = jnp.where(keep, pltpu.roll(bb, k, 0), 0.0)
        bb = bb + a * b_s
        a = a * a_s
        k *= 2
    hcar = carry_ref[d:d + 1, :]
    h = bb + a * hcar
    out_ref[...] = h
    carry_ref[d:d + 1, :] = h[0:1, :] if reverse else h[TM - 1:TM, :]


def _lru_kernel(pf_ref, cf_ref, nf_ref, pb_ref, cbk_ref, nb_ref, cw_ref, cb_ref, wa_ref, ba_ref, wi_ref, bi_ref, lam_ref,
                hf_ref, hb_ref, carry_ref, *, n_chunks):
    s = pl.program_id(1)

    @pl.when(s == 0)
    def _():
        carry_ref[...] = jnp.zeros_like(carry_ref)

    params = (cw_ref, cb_ref, wa_ref, ba_ref, wi_ref, bi_ref, lam_ref, carry_ref)
    _lru_dir(pf_ref, cf_ref, nf_ref, s, n_chunks, 0, False, *params, hf_ref)
    cb_idx = jnp.where(s == 0, 0, n_chunks - s)
    _lru_dir(pb_ref, cbk_ref, nb_ref, cb_idx, n_chunks, 1, True, *params, hb_ref)


def _lru(ux, w):
    nb, t, _ = ux.shape
    nt = t // TM
    fwd = lambda s: s
    bwd = lambda s: jnp.where(s == 0, 0, nt - s)
    prev = lambda c: jnp.maximum(c - 1, 0)
    nxt = lambda c: jnp.minimum(c + 1, nt - 1)
    blk = (None, TM, D_RNN)
    const2 = lambda b, s: (0, 0)
    const3 = lambda b, s: (0, 0, 0)
    const4 = lambda b, s: (0, 0, 0, 0)
    return pl.pallas_call(
        functools.partial(_lru_kernel, n_chunks=nt),
        grid=(nb, nt),
        in_specs=[
            pl.BlockSpec(blk, lambda b, s: (b, prev(fwd(s)), 0)),
            pl.BlockSpec(blk, lambda b, s: (b, fwd(s), 0)),
            pl.BlockSpec(blk, lambda b, s: (b, nxt(fwd(s)), 0)),
            pl.BlockSpec(blk, lambda b, s: (b, prev(bwd(s)), 0)),
            pl.BlockSpec(blk, lambda b, s: (b, bwd(s), 0)),
            pl.BlockSpec(blk, lambda b, s: (b, nxt(bwd(s)), 0)),
            _resident((4, D_RNN), const2),
            _resident((1, D_RNN), const2),
            _resident((2, LRU_BLOCKS, LRU_BW, LRU_BW), const4),
            _resident((2, 1, D_RNN), const3),
            _resident((2, LRU_BLOCKS, LRU_BW, LRU_BW), const4),
            _resident((2, 1, D_RNN), const3),
            _resident((2, 1, D_RNN), const3),
        ],
        out_specs=[
            pl.BlockSpec(blk, lambda b, s: (b, fwd(s), 0)),
            pl.BlockSpec(blk, lambda b, s: (b, bwd(s), 0)),
        ],
        out_shape=[jax.ShapeDtypeStruct((nb, t, D_RNN), F32)] * 2,
        scratch_shapes=[pltpu.VMEM((8, D_RNN), F32)],
        compiler_params=_cparams(("arbitrary", "arbitrary")),
        name="lru_scan",
    )(ux, ux, ux, ux, ux, ux, w["conv_w"], w["conv_b"], w["w_a"], w["b_a"], w["w_i"], w["b_i"], w["lam"])


def _softmax_av(q, k, v):
    s = lax.dot_general(q, k, (((1,), (1,)), ((), ())), preferred_element_type=F32)
    m = jnp.max(s, axis=-1, keepdims=True)
    p = jnp.exp(s - m)
    l = jnp.sum(p, axis=-1, keepdims=True)
    o = jnp.dot(p.astype(BF16), v, preferred_element_type=F32)
    return o / l


def _mla_attn_kernel(q_ref, k_ref, v_ref, o_ref):
    i = pl.program_id(2)

    @pl.when(i == 0)
    def _():
        o_ref[...] = _softmax_av(q_ref[...], k_ref[0:TM, :], v_ref[0:TM, :]).astype(BF16)

    @pl.when(i > 0)
    def _():
        o_ref[...] = _softmax_av(q_ref[...], k_ref[...], v_ref[...]).astype(BF16)


def _mla_attn(q, k, v):
    nb, t, _ = q.shape
    nt = t // TM
    return pl.pallas_call(
        _mla_attn_kernel,
        grid=(nb, MLA_H, nt),
        in_specs=[
            pl.BlockSpec((None, TM, 256), lambda b, h, i: (b, i, h)),
            pl.BlockSpec((None, t, 256), lambda b, h, i: (b, 0, h)),
            pl.BlockSpec((None, t, MLA_V), lambda b, h, i: (b, 0, h)),
        ],
        out_specs=pl.BlockSpec((None, TM, MLA_V), lambda b, h, i: (b, i, h)),
        out_shape=jax.ShapeDtypeStruct((nb, t, MLA_H * MLA_V), BF16),
        compiler_params=_cparams(("arbitrary", "arbitrary", "arbitrary")),
        name="mla_attn",
    )(q, k, v)


def _route(h2, rwh_ref, rwl_ref, rb_ref, route_ref):
    hi = h2.astype(BF16)
    lo = (h2 - hi.astype(F32)).astype(BF16)
    nt_dims = (((1,), (1,)), ((), ()))
    logits = (lax.dot_general(rwh_ref[...], hi, nt_dims, preferred_element_type=F32)
              + lax.dot_general(rwh_ref[...], lo, nt_dims, preferred_element_type=F32)
              + lax.dot_general(rwl_ref[...], hi, nt_dims, preferred_element_type=F32))
    scores = jax.nn.sigmoid(logits)
    sel = scores + rb_ref[...]
    sc = [scores[e:e + 1, :] for e in range(N_EXP)]
    se = [sel[e:e + 1, :] for e in range(N_EXP)]
    gs = []
    for g in range(N_GROUPS):
        a, b, c, d = se[4 * g:4 * g + 4]
        hi_ab, lo_ab = jnp.maximum(a, b), jnp.minimum(a, b)
        hi_cd, lo_cd = jnp.maximum(c, d), jnp.minimum(c, d)
        top1 = jnp.maximum(hi_ab, hi_cd)
        top2 = jnp.maximum(jnp.maximum(lo_ab, lo_cd), jnp.minimum(hi_ab, hi_cd))
        gs.append(top1 + top2)
    best = jnp.zeros_like(gs[0])
    best_v = gs[0]
    for g in range(1, N_GROUPS):
        upd = gs[g] > best_v
        best = jnp.where(upd, float(g), best)
        best_v = jnp.where(upd, gs[g], best_v)
    masked = [jnp.where(best == float(e // EXP_PER_GROUP), se[e], NEG) for e in range(N_EXP)]
    i1 = jnp.zeros_like(best)
    v1 = masked[0]
    s1 = sc[0]
    for e in range(1, N_EXP):
        upd = masked[e] > v1
        i1 = jnp.where(upd, float(e), i1)
        v1 = jnp.where(upd, masked[e], v1)
        s1 = jnp.where(upd, sc[e], s1)
    i2 = jnp.zeros_like(best)
    v2 = jnp.full_like(v1, -jnp.inf)
    s2 = jnp.zeros_like(s1)
    for e in range(N_EXP):
        upd = jnp.logical_and(masked[e] > v2, i1 != float(e))
        i2 = jnp.where(upd, float(e), i2)
        v2 = jnp.where(upd, masked[e], v2)
        s2 = jnp.where(upd, sc[e], s2)
    tot = s1 + s2
    route_ref[0:1, :] = i1
    route_ref[1:2, :] = i2
    route_ref[2:3, :] = s1 / tot
    route_ref[3:4, :] = s2 / tot
    route_ref[4:8, :] = jnp.zeros((4, TM), F32)


def _finish_mixer(y, x_ref, mod_ref, g_ref, rwh_ref, rwl_ref, rb_ref, xn_ref, h2_ref, route_ref):
    xn = x_ref[...] + mod_ref[:, 2 * D:3 * D] * y
    xn_ref[...] = xn
    h2 = _modulate(xn, g_ref[...], mod_ref[:, 3 * D:4 * D], mod_ref[:, 4 * D:5 * D])
    h2_ref[...] = h2
    _route(h2, rwh_ref, rwl_ref, rb_ref, route_ref)


def _ab_out_kernel(hf_ref, hb_ref, gate_ref, att_ref, x_ref, mod_ref, g_ref, wo_ref, rwh_ref, rwl_ref, rb_ref,
                   xn_ref, h2_ref, route_ref):
    rnn = ((hf_ref[...] + hb_ref[...]) * gate_ref[...].astype(F32)).astype(BF16)
    y = (jnp.dot(rnn, wo_ref[0:D_RNN, :], preferred_element_type=F32)
         + jnp.dot(att_ref[...], wo_ref[D_RNN:, :], preferred_element_type=F32))
    _finish_mixer(y, x_ref, mod_ref, g_ref, rwh_ref, rwl_ref, rb_ref, xn_ref, h2_ref, route_ref)


def _na_out_kernel(att_ref, x_ref, mod_ref, g_ref, wo_ref, rwh_ref, rwl_ref, rb_ref, xn_ref, h2_ref, route_ref):
    y = jnp.dot(att_ref[...], wo_ref[...], preferred_element_type=F32)
    _finish_mixer(y, x_ref, mod_ref, g_ref, rwh_ref, rwl_ref, rb_ref, xn_ref, h2_ref, route_ref)


def _mixer_out(kernel, acts, x_all, mod, g, wo, rw_hi, rw_lo, rb, off):
    nb, t, _ = x_all.shape
    nt = t // TM - off
    row = lambda b, i: (b, i + off, 0)
    out_row = lambda b, i: (b, i, 0)
    const = lambda b, i: (0, 0)
    act_specs = [pl.BlockSpec((None, TM, a.shape[2]), row if a.shape[1] == t else out_row) for a in acts]
    return pl.pallas_call(
        kernel,
        grid=(nb, nt),
        in_specs=act_specs + [
            pl.BlockSpec((None, TM, D), row),
            _mod_spec(nb, off),
            _resident((1, D), const),
            _resident(wo.shape, const),
            _resident((N_EXP, D), const),
            _resident((N_EXP, D), const),
            _resident((N_EXP, 1), const),
        ],
        out_specs=[
            pl.BlockSpec((None, TM, D), out_row),
            pl.BlockSpec((None, TM, D), out_row),
            pl.BlockSpec((None, 8, TM), lambda b, i: (b, 0, i)),
        ],
        out_shape=[
            jax.ShapeDtypeStruct((nb, nt * TM, D), F32),
            jax.ShapeDtypeStruct((nb, nt * TM, D), F32),
            jax.ShapeDtypeStruct((nb, 8, nt * TM), F32),
        ],
        compiler_params=_cparams(("arbitrary", "arbitrary")),
        name="mixer_out",
    )(*acts, x_all, mod, g, wo, rw_hi, rw_lo, rb)


def _na_in_kernel(x_ref, g_ref, mod_ref, w_ref, hg_ref, o_ref):
    p = pl.program_id(0)
    h = _modulate(x_ref[...], g_ref[...], mod_ref[:, 0:D], mod_ref[:, D:2 * D]).astype(BF16)
    u = jnp.dot(h, w_ref[...], preferred_element_type=F32)

    @pl.when(p < 2)
    def _():
        hg = hg_ref[...]
        for hh in range(NA_H):
            uh = u[:, hh * NA_DH:(hh + 1) * NA_DH]
            o_ref[:, hh * NA_DH:(hh + 1) * NA_DH] = _rms(uh, hg, NA_DH).astype(BF16)

    @pl.when(p == 2)
    def _():
        o_ref[...] = u.astype(BF16)


def _na_in(x_all, g, mod, w_qkv, hg):
    nb, t, _ = x_all.shape
    nt = t // TM
    return pl.pallas_call(
        _na_in_kernel,
        grid=(3, nb, nt),
        in_specs=[
            pl.BlockSpec((None, TM, D), lambda p, b, i: (b, i, 0)),
            _resident((1, D), lambda p, b, i: (0, 0)),
            pl.BlockSpec((None, 1, ADA * D), lambda p, b, i: (jnp.where(i == 0, nb, b), 0, 0)),
            pl.BlockSpec((D, D), lambda p, b, i: (0, p)),
            pl.BlockSpec((None, 1, NA_DH), lambda p, b, i: (p, 0, 0)),
        ],
        out_specs=pl.BlockSpec((None, None, TM, D), lambda p, b, i: (p, b, i, 0)),
        out_shape=jax.ShapeDtypeStruct((3, nb, t, D), BF16),
        compiler_params=_cparams(("arbitrary", "arbitrary", "arbitrary")),
        name="na_in",
    )(x_all, g, mod, w_qkv, hg)


def _na_key_base(rb, rows):
    return jnp.clip(rb * NA_QR - NA_ROWS // 2, 0, rows - NA_KR)


def _na_attn_kernel(var_ref, q_ref, k_ref, v_ref, bias_ref, o_ref, *, rows):
    rb = pl.program_id(2)
    start = pl.multiple_of(TM + _na_key_base(rb, rows) * GRID_W, GRID_W)
    nk = NA_KR * GRID_W
    q = q_ref[...]
    nt_dims = (((1,), (1,)), ((), ()))
    s_c = lax.dot_general(q, k_ref[0:TM, :], nt_dims, preferred_element_type=F32)
    s_l = lax.dot_general(q, k_ref[pl.ds(start, nk), :], nt_dims, preferred_element_type=F32) + bias_ref[...]
    m = jnp.maximum(jnp.max(s_c, axis=-1, keepdims=True), jnp.max(s_l, axis=-1, keepdims=True))
    p_c = jnp.exp(s_c - m)
    p_l = jnp.exp(s_l - m)
    l = jnp.sum(p_c, axis=-1, keepdims=True) + jnp.sum(p_l, axis=-1, keepdims=True)
    o = (jnp.dot(p_l.astype(BF16), v_ref[pl.ds(start, nk), :], preferred_element_type=F32)
         + jnp.dot(p_c.astype(BF16), v_ref[0:TM, :], preferred_element_type=F32))
    o_ref[...] = (o / l).astype(BF16)


def _na_attn(qkv, bias, variant):
    _, nb, t, _ = qkv.shape
    rows = (t - TM) // GRID_W
    nrb = rows // NA_QR
    nq = NA_QR * GRID_W
    assert nq == TM
    return pl.pallas_call(
        functools.partial(_na_attn_kernel, rows=rows),
        grid_spec=pltpu.PrefetchScalarGridSpec(
            num_scalar_prefetch=1,
            grid=(NA_H, nb, nrb),
            in_specs=[
                pl.BlockSpec((None, None, nq, NA_DH), lambda h, b, r, var: (0, b, r + 1, h)),
                pl.BlockSpec((None, None, t, NA_DH), lambda h, b, r, var: (1, b, 0, h)),
                pl.BlockSpec((None, None, t, NA_DH), lambda h, b, r, var: (2, b, 0, h)),
                pl.BlockSpec((None, None, nq, NA_KR * GRID_W), lambda h, b, r, var: (h, var[r], 0, 0)),
            ],
            out_specs=pl.BlockSpec((None, nq, NA_DH), lambda h, b, r, var: (b, r, h)),
        ),
        out_shape=jax.ShapeDtypeStruct((nb, t - TM, D), BF16),
        compiler_params=_cparams(("arbitrary", "arbitrary", "arbitrary")),
        name="na_attn",
    )(variant, qkv, qkv, qkv, bias)


def _na_bias_tables(rpb, rows):
    nrb = rows // NA_QR
    rb = np.arange(nrb)
    kb = np.clip(rb * NA_QR - NA_ROWS // 2, 0, rows - NA_KR)
    r_q = rb[:, None] * NA_QR + np.arange(NA_QR)[None, :]
    r0_q = np.clip(r_q - NA_ROWS // 2, 0, rows - NA_ROWS)
    sig = np.concatenate([r_q - kb[:, None], r0_q - kb[:, None]], axis=1)
    uniq, variant = np.unique(sig, axis=0, return_inverse=True)
    variant = np.asarray(variant).reshape(-1)
    kr = np.arange(NA_KR)[None, :]
    cols = np.arange(GRID_W)
    col_start = np.clip(cols - NA_COLS // 2, 0, GRID_W - NA_COLS)
    kc = cols[None, :]
    col_ok = (kc >= col_start[:, None]) & (kc < col_start[:, None] + NA_COLS)
    dcol = np.clip(kc - cols[:, None] + NA_COLS - 1, 0, 2 * NA_COLS - 2)
    tabs = []
    for u in uniq:
        r_abs = u[:NA_QR][:, None]
        r0 = u[NA_QR:][:, None]
        k_abs = kr
        row_ok = (k_abs >= r0) & (k_abs < r0 + NA_ROWS)
        drow = np.clip(k_abs - r_abs + NA_ROWS - 1, 0, 2 * NA_ROWS - 2)
        ok = row_ok[:, None, :, None] & col_ok[None, :, None, :]
        bias = rpb[:, drow[:, None, :, None], dcol[None, :, None, :]]
        bias = jnp.where(ok[None], bias, NEG)
        tabs.append(bias.reshape(rpb.shape[0], NA_QR * GRID_W, NA_KR * GRID_W))
    return jnp.stack(tabs, axis=1).astype(F32), jnp.asarray(variant, I32)


def _dispatch_kernel(pos_ref, padpos_ref, h2_ref, xs_ref, zero_ref, sem, *, nt):
    tile = pl.program_id(0) * nt + pl.program_id(1)

    def pad_copy(q):
        return pltpu.make_async_copy(zero_ref.at[pl.ds(0, 1), :], xs_ref.at[pl.ds(padpos_ref[q], 1), :], sem)

    @pl.when(tile == 0)
    def _():
        zero_ref[...] = jnp.zeros_like(zero_ref)

        def start(q, c):
            pad_copy(q).start()
            return c

        def wait(q, c):
            pad_copy(q).wait()
            return c

        lax.fori_loop(0, N_EXP * TM, start, 0)
        lax.fori_loop(0, N_EXP * TM, wait, 0)

    base = tile * (2 * TM)

    def row_copy(j, kk):
        return pltpu.make_async_copy(h2_ref.at[pl.ds(j, 1), :], xs_ref.at[pl.ds(pos_ref[base + 2 * j + kk], 1), :], sem)

    def start(j, c):
        row_copy(j, 0).start()
        row_copy(j, 1).start()
        return c

    def wait(j, c):
        row_copy(j, 0).wait()
        row_copy(j, 1).wait()
        return c

    lax.fori_loop(0, TM, start, 0)
    lax.fori_loop(0, TM, wait, 0)


def _dispatch(h2, pos, padpos, n_rows):
    nb, t, _ = h2.shape
    nt = t // TM
    return pl.pallas_call(
        functools.partial(_dispatch_kernel, nt=nt),
        grid_spec=pltpu.PrefetchScalarGridSpec(
            num_scalar_prefetch=2,
            grid=(nb, nt),
            in_specs=[pl.BlockSpec((TM, D), lambda b, i, *_: (b * nt + i, 0))],
            out_specs=pl.BlockSpec(memory_space=pl.ANY),
            scratch_shapes=[pltpu.VMEM((8, D), F32), pltpu.SemaphoreType.DMA],
        ),
        out_shape=jax.ShapeDtypeStruct((n_rows, D), F32),
        compiler_params=_cparams(("arbitrary", "arbitrary")),
        name="moe_dispatch",
    )(pos, padpos, h2.reshape(nb * t, D))


def _swiglu_tile(x_bf16, wgu, wd):
    gu = jnp.dot(x_bf16, wgu, preferred_element_type=F32)
    hmid = (_silu(gu[:, 0:D_EXP]) * gu[:, D_EXP:]).astype(BF16)
    return jnp.dot(hmid, wd, preferred_element_type=F32)


def _expert_kernel(te_ref, tb_ref, nu_ref, x_ref, wgu_ref, wd_ref, y_ref):
    g = pl.program_id(0)

    @pl.when(g < nu_ref[0])
    def _():
        y_ref[...] = _swiglu_tile(x_ref[...].astype(BF16), wgu_ref[...], wd_ref[...])

    @pl.when(g >= nu_ref[0])
    def _():
        y_ref[...] = jnp.zeros_like(y_ref)


def _experts(xs, tile_expert, tile_block, n_used, wgu, wd):
    n_rows = xs.shape[0]
    n_tiles = n_rows // TM
    return pl.pallas_call(
        _expert_kernel,
        grid_spec=pltpu.PrefetchScalarGridSpec(
            num_scalar_prefetch=3,
            grid=(n_tiles,),
            in_specs=[
                pl.BlockSpec((TM, D), lambda g, te, tb, nu: (tb[g], 0)),
                pl.BlockSpec((None, D, 2 * D_EXP), lambda g, te, tb, nu: (te[g], 0, 0)),
                pl.BlockSpec((None, D_EXP, D), lambda g, te, tb, nu: (te[g], 0, 0)),
            ],
            out_specs=pl.BlockSpec((TM, D), lambda g, te, tb, nu: (g, 0)),
        ),
        out_shape=jax.ShapeDtypeStruct((n_rows, D), F32),
        compiler_params=_cparams(("arbitrary",)),
        name="moe_experts",
    )(tile_expert, tile_block, n_used, xs, wgu, wd)


def _combine_kernel(pos_ref, h2_ref, xn_ref, mod_ref, wt_ref, wsgu_ref, wsd_ref, ys_ref, o_ref, ybuf, sem, *, nt):
    tile = pl.program_id(0) * nt + pl.program_id(1)
    base = tile * (2 * TM)

    def row_copy(j, kk):
        return pltpu.make_async_copy(ys_ref.at[pl.ds(pos_ref[base + 2 * j + kk], 1), :], ybuf.at[kk, pl.ds(j, 1), :], sem)

    def start(j, c):
        row_copy(j, 0).start()
        row_copy(j, 1).start()
        return c

    def wait(j, c):
        row_copy(j, 0).wait()
        row_copy(j, 1).wait()
        return c

    lax.fori_loop(0, TM, start, 0)
    shared = _swiglu_tile(h2_ref[...].astype(BF16), wsgu_ref[...], wsd_ref[...])
    lax.fori_loop(0, TM, wait, 0)
    wt = wt_ref[...]
    y = shared + wt[:, 0:1] * ybuf[0] + wt[:, 1:2] * ybuf[1]
    o_ref[...] = xn_ref[...] + mod_ref[:, 5 * D:6 * D] * y


def _combine(pos, h2, xn, mod, wt, wsgu, wsd, ys, off):
    nb, t, _ = h2.shape
    nt = t // TM
    row = lambda b, i, *_: (b, i, 0)
    const = lambda b, i, *_: (0, 0)
    return pl.pallas_call(
        functools.partial(_combine_kernel, nt=nt),
        grid_spec=pltpu.PrefetchScalarGridSpec(
            num_scalar_prefetch=1,
            grid=(nb, nt),
            in_specs=[
                pl.BlockSpec((None, TM, D), row),
                pl.BlockSpec((None, TM, D), row),
                pl.BlockSpec((None, 1, ADA * D), lambda b, i, *_: (jnp.where(i + off == 0, nb, b), 0, 0)),
                pl.BlockSpec((None, TM, 8), row),
                _resident((D, 2 * D_EXP), const),
                _resident((D_EXP, D), const),
                pl.BlockSpec(memory_space=pl.ANY),
            ],
            out_specs=pl.BlockSpec((None, TM, D), row),
            scratch_shapes=[pltpu.VMEM((2, TM, D), F32), pltpu.SemaphoreType.DMA],
        ),
        out_shape=jax.ShapeDtypeStruct((nb, t, D), F32),
        compiler_params=_cparams(("arbitrary", "arbitrary")),
        name="moe_combine",
    )(pos, h2, xn, mod, wt, wsgu, wsd, ys)


def _moe_plan(route):
    nb, _, tp = route.shape
    n_tok = nb * tp
    idx = route[:, 0:2, :].astype(I32).transpose(0, 2, 1).reshape(n_tok * 2)
    wt = jnp.pad(route[:, 2:4, :].transpose(0, 2, 1), ((0, 0), (0, 0), (0, 6)))
    onehot = (idx[:, None] == jnp.arange(N_EXP, dtype=I32)[None, :]).astype(I32)
    csum = jnp.cumsum(onehot, axis=0)
    counts = csum[-1]
    rank = jnp.sum((csum - onehot) * onehot, axis=1)
    padded = ((counts + TM - 1) // TM) * TM
    ends = jnp.cumsum(padded)
    starts = ends - padded
    pos = (jnp.sum(onehot * starts[None, :], axis=1) + rank).astype(I32)
    n_tiles = (2 * n_tok) // TM + N_EXP
    n_used = (ends[-1] // TM).astype(I32)
    tile_block = jnp.minimum(jnp.arange(n_tiles, dtype=I32), n_used - 1)
    tile_expert = jnp.sum((tile_block[:, None] * TM >= ends[None, :]).astype(I32), axis=1).astype(I32)
    npads = padded - counts
    pad_ends = jnp.cumsum(npads)
    pad_starts = pad_ends - npads
    qidx = jnp.arange(N_EXP * TM, dtype=I32)
    pe = jnp.minimum(jnp.sum((qidx[:, None] >= pad_ends[None, :]).astype(I32), axis=1), N_EXP - 1)
    in_seg = starts[pe] + counts[pe] + (qidx - pad_starts[pe])
    tail = ends[-1] + (qidx - pad_ends[-1])
    padpos = jnp.where(qidx < pad_ends[-1], in_seg, tail).astype(I32)
    return pos, padpos, tile_expert, tile_block, n_used.reshape(1), wt, n_tiles * TM


def _moe(h2, xn, route, mod, wgu, wd, wsgu, wsd, off):
    pos, padpos, tile_expert, tile_block, n_used, wt, n_rows = _moe_plan(route)
    xs = _dispatch(h2, pos, padpos, n_rows)
    ys = _experts(xs, tile_expert, tile_block, n_used, wgu, wd)
    return _combine(pos, h2, xn, mod, wt, wsgu, wsd, ys, off)


def _rope_tables(t, s):
    half = MLA_ROPE // 2
    inv_freq = ROPE_THETA ** (-jnp.arange(0, half, 2, dtype=F32) / half)
    tt = jnp.arange(t - s)
    ang_row = (tt // GRID_W).astype(F32)[:, None] * inv_freq
    ang_col = (tt % GRID_W).astype(F32)[:, None] * inv_freq
    ang = jnp.concatenate([ang_row] * 2 + [ang_col] * 2, axis=1)
    cos = jnp.cos(ang)
    sin = jnp.sin(ang)
    sign = np.where((np.arange(MLA_ROPE) % 32) < 16, -1.0, 1.0).astype(np.float32)
    sin = sin * sign[None, :]
    cos = jnp.concatenate([jnp.ones((s, MLA_ROPE), F32), cos], axis=0)
    sin = jnp.concatenate([jnp.zeros((s, MLA_ROPE), F32), sin], axis=0)
    cos = jnp.pad(cos, ((0, 0), (0, LANE - MLA_ROPE)), constant_values=1.0)
    sin = jnp.pad(sin, ((0, 0), (0, LANE - MLA_ROPE)))
    return cos.astype(F32), sin.astype(F32)


def _ab_weights(j, t, s, ab_w_in, lru_conv_w, lru_conv_b, lru_w_a, lru_b_a, lru_w_i, lru_b_i, lru_lambda,
                mla_q_norm_g, mla_w_uq, mla_kv_norm_g, mla_w_ukv, mla_q_head_g, mla_k_head_g):
    w_in = jnp.pad(ab_w_in[j], ((0, 0), (0, AB_IN_PAD - AB_IN))).astype(BF16)
    uq = mla_w_uq[j].reshape(Q_RANK, MLA_H, MLA_QK)
    uq_n = uq[:, :, :MLA_NOPE].reshape(Q_RANK, MLA_H * LANE)
    uq_r = jnp.pad(uq[:, :, MLA_NOPE:], ((0, 0), (0, 0), (0, LANE - MLA_ROPE))).reshape(Q_RANK, MLA_H * LANE)
    ukv = mla_w_ukv[j].reshape(KV_RANK, MLA_H, MLA_NOPE + MLA_V)
    ukv_k = ukv[:, :, :MLA_NOPE].reshape(KV_RANK, MLA_H * LANE)
    ukv_v = ukv[:, :, MLA_NOPE:].reshape(KV_RANK, MLA_H * MLA_V)
    scale = MLA_QK ** -0.5
    qg = mla_q_head_g[j] * scale
    kg = mla_k_head_g[j]
    pad_r = lambda g: jnp.pad(g[MLA_NOPE:], (0, LANE - MLA_ROPE))
    hg = jnp.stack([qg[:MLA_NOPE], pad_r(qg), kg[:MLA_NOPE], pad_r(kg)], axis=0)
    cos, sin = _rope_tables(t, s)
    return dict(
        w_in=w_in, qng=mla_q_norm_g[j][None], kvng=mla_kv_norm_g[j][None],
        w_uq=jnp.concatenate([uq_n, uq_r], axis=1).astype(BF16),
        w_ukv=jnp.concatenate([ukv_k, ukv_v], axis=1).astype(BF16),
        hg=hg, cos=cos, sin=sin,
        conv_w=lru_conv_w[j], conv_b=lru_conv_b[j][None],
        w_a=lru_w_a[j].astype(BF16), b_a=lru_b_a[j][:, None, :],
        w_i=lru_w_i[j].astype(BF16), b_i=lru_b_i[j][:, None, :], lam=lru_lambda[j][:, None, :],
    )


def kernel(x, c, ctx, c_ctx, ada_w, ada_b, norm_mix_g, norm_ffn_g, ab_w_in, ab_w_out, lru_conv_w, lru_conv_b, lru_w_a, lru_b_a, lru_w_i, lru_b_i, lru_lambda, mla_q_norm_g, mla_w_uq, mla_kv_norm_g, mla_w_ukv, mla_q_head_g, mla_k_head_g, na_w_qkv, na_w_out, na_q_head_g, na_k_head_g, na_rpb, router_w, router_b, moe_w_gate, moe_w_up, moe_w_down, moe_ws_gate, moe_ws_up, moe_ws_down):
    nb, seq, _ = x.shape
    s = ctx.shape[1]
    depth = ada_w.shape[0]
    assert s == TM and seq % TM == 0 and nb < 8
    t = s + seq
    rows = seq // GRID_W

    cs = jnp.zeros((8, D), F32).at[:nb].set(c).at[nb].set(c_ctx)
    mod_all = _adaln(cs, ada_w, ada_b).reshape(depth, 8, 1, ADA * D)

    rw_t = router_w.T
    rw_hi = rw_t.astype(BF16)
    rw_lo = (rw_t - rw_hi.astype(F32)).astype(BF16)
    rb = router_b.reshape(N_EXP, 1)

    x_all = jnp.concatenate([ctx, x], axis=1)
    for i in range(depth):
        last = i == depth - 1
        off = 1 if last else 0
        j = i // 2
        mod = mod_all[i]
        g_mix = norm_mix_g[i][None]
        g_ffn = norm_ffn_g[i][None]
        if i % 2 == 0:
            w = _ab_weights(j, t, s, ab_w_in, lru_conv_w, lru_conv_b, lru_w_a, lru_b_a, lru_w_i, lru_b_i, lru_lambda,
                            mla_q_norm_g, mla_w_uq, mla_kv_norm_g, mla_w_ukv, mla_q_head_g, mla_k_head_g)
            ux, gate, q, k, v = _ab_in(x_all, g_mix, mod, w)
            h_f, h_b = _lru(ux, w)
            att = _mla_attn(q, k, v)
            xn, h2, route = _mixer_out(_ab_out_kernel, (h_f, h_b, gate, att),
                                       x_all, mod, g_ffn, ab_w_out[j].astype(BF16), rw_hi, rw_lo, rb, off)
        else:
            scale = NA_DH ** -0.5
            hg = jnp.stack([na_q_head_g[j] * scale, na_k_head_g[j], jnp.ones((NA_DH,), F32)], axis=0)[:, None, :]
            qkv = _na_in(x_all, g_mix, mod, na_w_qkv[j].astype(BF16), hg)
            bias, variant = _na_bias_tables(na_rpb[j], rows)
            att = _na_attn(qkv, bias, variant)
            assert last, "the neighbourhood layer has no context-output path"
            xn, h2, route = _mixer_out(_na_out_kernel, (att,), x_all, mod, g_ffn, na_w_out[j].astype(BF16),
                                       rw_hi, rw_lo, rb, off)
        wgu = jnp.concatenate([moe_w_gate[i], moe_w_up[i]], axis=-1).astype(BF16)
        wd = moe_w_down[i].astype(BF16)
        wsgu = jnp.concatenate([moe_ws_gate[i], moe_ws_up[i]], axis=-1).astype(BF16)
        wsd = moe_ws_down[i].astype(BF16)
        x_all = _moe(h2, xn, route, mod, wgu, wd, wsgu, wsd, off)
    return x_all
```

```python
import functools

import numpy as np
import jax
import jax.numpy as jnp
from jax import lax
from jax.experimental import pallas as pl
from jax.experimental.pallas import tpu as pltpu

F32 = jnp.float32
BF16 = jnp.bfloat16
I32 = jnp.int32

D = 2048
ADA = 6
EPS = 1e-6
NEG = -1e30
GRID_W = 64
D_RNN = 1024
LRU_BLOCKS = 8
LRU_BW = 128
LRU_C = 8.0
MLA_H = 8
MLA_NOPE = 128
MLA_ROPE = 64
MLA_QK = 192
MLA_V = 128
Q_RANK = 512
KV_RANK = 512
ROPE_THETA = 10000.0
AB_IN = 3136
AB_IN_PAD = 3200
NA_H = 16
NA_DH = 128
NA_ROWS = 8
NA_COLS = 16
N_EXP = 16
N_GROUPS = 4
EXP_PER_GROUP = 4
D_EXP = 1408

LANE = 128
TM = 256
VMEM_LIMIT = 56 * 1024 * 1024

NA_QR = 4
NA_KR = 12


def _cparams(sem, vmem=VMEM_LIMIT):
    return pltpu.CompilerParams(dimension_semantics=sem, vmem_limit_bytes=vmem)


def _resident(shape, index_map):
    return pl.BlockSpec(shape, index_map, pipeline_mode=pl.Buffered(1))


def _rms(x, g, n):
    ms = jnp.sum(x * x, axis=-1, keepdims=True) * (1.0 / n)
    return (x * lax.rsqrt(ms + EPS)) * g


def _modulate(x, g, shift, scale):
    return _rms(x, g, D) * (1.0 + scale) + shift


def _gelu_tanh(x):
    c = 0.7978845608028654
    return x * (0.5 * (1.0 + jnp.tanh(c * (x + 0.044715 * (x * x * x)))))


def _silu(x):
    return x * jax.nn.sigmoid(x)


ADA_TN = 1536


def _adaln_kernel(cs_ref, w_ref, b_ref, o_ref):
    a = _silu(cs_ref[...]).astype(BF16)
    o_ref[...] = jnp.dot(a, w_ref[...].astype(BF16), preferred_element_type=F32) + b_ref[...]


def _adaln(cs, ada_w, ada_b):
    depth = ada_w.shape[0]
    n = ADA * D
    return pl.pallas_call(
        _adaln_kernel,
        grid=(depth, n // ADA_TN),
        in_specs=[
            pl.BlockSpec((8, D), lambda l, j: (0, 0)),
            pl.BlockSpec((None, D, ADA_TN), lambda l, j: (l, 0, j)),
            pl.BlockSpec((None, 1, ADA_TN), lambda l, j: (l, 0, j)),
        ],
        out_specs=pl.BlockSpec((None, 8, ADA_TN), lambda l, j: (l, 0, j)),
        out_shape=jax.ShapeDtypeStruct((depth, 8, n), F32),
        compiler_params=_cparams(("arbitrary", "arbitrary")),
        name="adaln",
    )(cs, ada_w, ada_b.reshape(depth, 1, n))


def _mod_spec(nb, off):
    return pl.BlockSpec((None, 1, ADA * D), lambda b, i: (jnp.where(i + off == 0, nb, b), 0, 0))


def _ab_in_kernel(x_ref, g_ref, mod_ref, win_ref, qng_ref, kvng_ref, wuq_ref, wukv_ref, hg_ref, cos_ref, sin_ref,
                  ux_ref, gate_ref, q_ref, k_ref, v_ref):
    h = _modulate(x_ref[...], g_ref[...], mod_ref[:, 0:D], mod_ref[:, D:2 * D]).astype(BF16)
    u = jnp.dot(h, win_ref[...], preferred_element_type=F32)
    ux_ref[...] = u[:, 0:D_RNN].astype(BF16)
    gate_ref[...] = _gelu_tanh(u[:, D_RNN:2 * D_RNN]).astype(BF16)
    o2 = 2 * D_RNN
    cq = _rms(u[:, o2:o2 + Q_RANK], qng_ref[...], Q_RANK).astype(BF16)
    ckv = _rms(u[:, o2 + Q_RANK:o2 + Q_RANK + KV_RANK], kvng_ref[...], KV_RANK).astype(BF16)
    qf = jnp.dot(cq, wuq_ref[...], preferred_element_type=F32)
    kvf = jnp.dot(ckv, wukv_ref[...], preferred_element_type=F32)
    kr = u[:, o2 + Q_RANK + KV_RANK:AB_IN_PAD]
    cos = cos_ref[...]
    sin = sin_ref[...]
    lane = lax.broadcasted_iota(I32, (TM, LANE), 1)
    first_half = (lane % 32) < 16

    def rope(z):
        partner = jnp.where(first_half, pltpu.roll(z, LANE - 16, 1), pltpu.roll(z, 16, 1))
        return z * cos + partner * sin

    hg = hg_ref[...]
    kr_ss = jnp.sum(kr * kr, axis=-1, keepdims=True)
    nh = MLA_H * MLA_NOPE
    for hh in range(MLA_H):
        qn = qf[:, hh * LANE:(hh + 1) * LANE]
        qr = qf[:, nh + hh * LANE:nh + (hh + 1) * LANE]
        ss = jnp.sum(qn * qn, axis=-1, keepdims=True) + jnp.sum(qr * qr, axis=-1, keepdims=True)
        inv = lax.rsqrt(ss * (1.0 / MLA_QK) + EPS)
        q_ref[:, hh * 256:hh * 256 + LANE] = ((qn * inv) * hg[0:1]).astype(BF16)
        q_ref[:, hh * 256 + LANE:(hh + 1) * 256] = rope((qr * inv) * hg[1:2]).astype(BF16)
        kn = kvf[:, hh * LANE:(hh + 1) * LANE]
        ss = jnp.sum(kn * kn, axis=-1, keepdims=True) + kr_ss
        inv = lax.rsqrt(ss * (1.0 / MLA_QK) + EPS)
        k_ref[:, hh * 256:hh * 256 + LANE] = ((kn * inv) * hg[2:3]).astype(BF16)
        k_ref[:, hh * 256 + LANE:(hh + 1) * 256] = rope((kr * inv) * hg[3:4]).astype(BF16)
    v_ref[...] = kvf[:, nh:].astype(BF16)


def _ab_in(x_all, g, mod, w):
    nb, t, _ = x_all.shape
    nt = t // TM
    row = lambda b, i: (b, i, 0)
    const = lambda b, i: (0, 0)
    outs = [
        jax.ShapeDtypeStruct((nb, t, D_RNN), BF16),
        jax.ShapeDtypeStruct((nb, t, D_RNN), BF16),
        jax.ShapeDtypeStruct((nb, t, MLA_H * 256), BF16),
        jax.ShapeDtypeStruct((nb, t, MLA_H * 256), BF16),
        jax.ShapeDtypeStruct((nb, t, MLA_H * MLA_V), BF16),
    ]
    return pl.pallas_call(
        _ab_in_kernel,
        grid=(nb, nt),
        in_specs=[
            pl.BlockSpec((None, TM, D), row),
            _resident((1, D), const),
            _mod_spec(nb, 0),
            _resident((D, AB_IN_PAD), const),
            _resident((1, Q_RANK), const),
            _resident((1, KV_RANK), const),
            _resident((Q_RANK, 2 * MLA_H * LANE), const),
            _resident((KV_RANK, 2 * MLA_H * LANE), const),
            _resident((4, LANE), const),
            pl.BlockSpec((TM, LANE), lambda b, i: (i, 0)),
            pl.BlockSpec((TM, LANE), lambda b, i: (i, 0)),
        ],
        out_specs=[
            pl.BlockSpec((None, TM, D_RNN), row),
            pl.BlockSpec((None, TM, D_RNN), row),
            pl.BlockSpec((None, TM, MLA_H * 256), row),
            pl.BlockSpec((None, TM, MLA_H * 256), row),
            pl.BlockSpec((None, TM, MLA_H * MLA_V), row),
        ],
        out_shape=outs,
        compiler_params=_cparams(("arbitrary", "arbitrary")),
        name="ab_in",
    )(x_all, g, mod, w["w_in"], w["qng"], w["kvng"], w["w_uq"], w["w_ukv"], w["hg"], w["cos"], w["sin"])


def _lru_dir(prev_ref, cur_ref, next_ref, c, n_chunks, d, reverse,
             cw_ref, cb_ref, wa_ref, ba_ref, wi_ref, bi_ref, lam_ref, carry_ref, out_ref):
    xc = cur_ref[...].astype(F32)
    row = lax.broadcasted_iota(I32, (TM, 1), 0)
    left_ok = jnp.where(c >= 2, 1.0, 0.0)
    right_ok = jnp.where(jnp.logical_and(c >= 1, c <= n_chunks - 2), 1.0, 0.0)
    p0 = prev_ref[TM - 2:TM - 1, :].astype(F32) * left_ok
    p1 = prev_ref[TM - 1:TM, :].astype(F32) * left_ok
    n0 = next_ref[0:1, :].astype(F32) * right_ok
    x_m2 = jnp.where(row >= 2, pltpu.roll(xc, 2, 0), jnp.where(row == 0, p0, p1))
    x_m1 = jnp.where(row >= 1, pltpu.roll(xc, 1, 0), p1)
    x_p1 = jnp.where(row <= TM - 2, pltpu.roll(xc, TM - 1, 0), n0)
    cw = cw_ref[...]
    xconv = x_m2 * cw[0:1] + x_m1 * cw[1:2] + xc * cw[2:3] + x_p1 * cw[3:4] + cb_ref[...]

    xb = xconv.astype(BF16)
    ra = []
    ri = []
    for n in range(LRU_BLOCKS):
        xs = xb[:, n * LRU_BW:(n + 1) * LRU_BW]
        ra.append(jnp.dot(xs, wa_ref[d, n], preferred_element_type=F32))
        ri.append(jnp.dot(xs, wi_ref[d, n], preferred_element_type=F32))
    r = jax.nn.sigmoid(jnp.concatenate(ra, axis=1) + ba_ref[d])
    gi = jax.nn.sigmoid(jnp.concatenate(ri, axis=1) + bi_ref[d])
    neg_lam = -lam_ref[d]
    softplus = jnp.maximum(neg_lam, 0.0) + jnp.log1p(jnp.exp(-jnp.abs(neg_lam)))
    log_a = (-LRU_C * r) * softplus
    a = jnp.exp(log_a)
    th = jnp.tanh(log_a)
    neg_expm1 = (-2.0 * th) / (1.0 - th)
    bb = jnp.sqrt(neg_expm1) * gi * xconv

    k = 1
    while k < TM:
        if reverse:
            keep = row < TM - k
            a_s = jnp.where(keep, pltpu.roll(a, TM - k, 0), 1.0)
            b_s = jnp.where(keep, pltpu.roll(bb, TM - k, 0), 0.0)
        else:
            keep = row >= k
            a_s = jnp.where(keep, pltpu.roll(a, k, 0), 1.0)
            b_s = jnp.where(keep, pltpu.roll(bb, k, 0), 0.0)
        bb = bb + a * b_s
        a = a * a_s
        k *= 2
    hcar = carry_ref[d:d + 1, :]
    h = bb + a * hcar
    out_ref[...] = h
    carry_ref[d:d + 1, :] = h[0:1, :] if reverse else h[TM - 1:TM, :]


def _lru_kernel(pf_ref, cf_ref, nf_ref, pb_ref, cbk_ref, nb_ref, cw_ref, cb_ref, wa_ref, ba_ref, wi_ref, bi_ref, lam_ref,
                hf_ref, hb_ref, carry_ref, *, n_chunks):
    s = pl.program_id(1)

    @pl.when(s == 0)
    def _():
        carry_ref[...] = jnp.zeros_like(carry_ref)

    params = (cw_ref, cb_ref, wa_ref, ba_ref, wi_ref, bi_ref, lam_ref, carry_ref)
    _lru_dir(pf_ref, cf_ref, nf_ref, s, n_chunks, 0, False, *params, hf_ref)
    cb_idx = jnp.where(s == 0, 0, n_chunks - s)
    _lru_dir(pb_ref, cbk_ref, nb_ref, cb_idx, n_chunks, 1, True, *params, hb_ref)


def _lru(ux, w):
    nb, t, _ = ux.shape
    nt = t // TM
    fwd = lambda s: s
    bwd = lambda s: jnp.where(s == 0, 0, nt - s)
    prev = lambda c: jnp.maximum(c - 1, 0)
    nxt = lambda c: jnp.minimum(c + 1, nt - 1)
    blk = (None, TM, D_RNN)
    const2 = lambda b, s: (0, 0)
    const3 = lambda b, s: (0, 0, 0)
    const4 = lambda b, s: (0, 0, 0, 0)
    return pl.pallas_call(
        functools.partial(_lru_kernel, n_chunks=nt),
        grid=(nb, nt),
        in_specs=[
            pl.BlockSpec(blk, lambda b, s: (b, prev(fwd(s)), 0)),
            pl.BlockSpec(blk, lambda b, s: (b, fwd(s), 0)),
            pl.BlockSpec(blk, lambda b, s: (b, nxt(fwd(s)), 0)),
            pl.BlockSpec(blk, lambda b, s: (b, prev(bwd(s)), 0)),
            pl.BlockSpec(blk, lambda b, s: (b, bwd(s), 0)),
            pl.BlockSpec(blk, lambda b, s: (b, nxt(bwd(s)), 0)),
            _resident((4, D_RNN), const2),
            _resident((1, D_RNN), const2),
            _resident((2, LRU_BLOCKS, LRU_BW, LRU_BW), const4),
            _resident((2, 1, D_RNN), const3),
            _resident((2, LRU_BLOCKS, LRU_BW, LRU_BW), const4),
            _resident((2, 1, D_RNN), const3),
            _resident((2, 1, D_RNN), const3),
        ],
        out_specs=[
            pl.BlockSpec(blk, lambda b, s: (b, fwd(s), 0)),
            pl.BlockSpec(blk, lambda b, s: (b, bwd(s), 0)),
        ],
        out_shape=[jax.ShapeDtypeStruct((nb, t, D_RNN), F32)] * 2,
        scratch_shapes=[pltpu.VMEM((8, D_RNN), F32)],
        compiler_params=_cparams(("arbitrary", "arbitrary")),
        name="lru_scan",
    )(ux, ux, ux, ux, ux, ux, w["conv_w"], w["conv_b"], w["w_a"], w["b_a"], w["w_i"], w["b_i"], w["lam"])


def _softmax_av(q, k, v):
    s = lax.dot_general(q, k, (((1,), (1,)), ((), ())), preferred_element_type=F32)
    m = jnp.max(s, axis=-1, keepdims=True)
    p = jnp.exp(s - m)
    l = jnp.sum(p, axis=-1, keepdims=True)
    o = jnp.dot(p.astype(BF16), v, preferred_element_type=F32)
    return o / l


MLA_HPS = 2


def _mla_attn_kernel(q_ref, k_ref, v_ref, o_ref):
    i = pl.program_id(2)

    def heads(n_keys):
        for hh in range(MLA_HPS):
            qs = slice(hh * 256, (hh + 1) * 256)
            vs = slice(hh * MLA_V, (hh + 1) * MLA_V)
            o_ref[:, vs] = _softmax_av(q_ref[:, qs], k_ref[0:n_keys, qs], v_ref[0:n_keys, vs]).astype(BF16)

    @pl.when(i == 0)
    def _():
        heads(TM)

    @pl.when(i > 0)
    def _():
        heads(k_ref.shape[0])


def _mla_attn(q, k, v):
    nb, t, _ = q.shape
    nt = t // TM
    return pl.pallas_call(
        _mla_attn_kernel,
        grid=(nb, MLA_H // MLA_HPS, nt),
        in_specs=[
            pl.BlockSpec((None, TM, MLA_HPS * 256), lambda b, h, i: (b, i, h)),
            pl.BlockSpec((None, t, MLA_HPS * 256), lambda b, h, i: (b, 0, h)),
            pl.BlockSpec((None, t, MLA_HPS * MLA_V), lambda b, h, i: (b, 0, h)),
        ],
        out_specs=pl.BlockSpec((None, TM, MLA_HPS * MLA_V), lambda b, h, i: (b, i, h)),
        out_shape=jax.ShapeDtypeStruct((nb, t, MLA_H * MLA_V), BF16),
        compiler_params=_cparams(("arbitrary", "arbitrary", "arbitrary")),
        name="mla_attn",
    )(q, k, v)


def _route(h2, rwh_ref, rwl_ref, rb_ref, route_ref):
    hi = h2.astype(BF16)
    lo = (h2 - hi.astype(F32)).astype(BF16)
    nt_dims = (((1,), (1,)), ((), ()))
    logits = (lax.dot_general(rwh_ref[...], hi, nt_dims, preferred_element_type=F32)
              + lax.dot_general(rwh_ref[...], lo, nt_dims, preferred_element_type=F32)
              + lax.dot_general(rwl_ref[...], hi, nt_dims, preferred_element_type=F32))
    scores = jax.nn.sigmoid(logits)
    sel = scores + rb_ref[...]
    sc = [scores[e:e + 1, :] for e in range(N_EXP)]
    se = [sel[e:e + 1, :] for e in range(N_EXP)]
    gs = []
    for g in range(N_GROUPS):
        a, b, c, d = se[4 * g:4 * g + 4]
        hi_ab, lo_ab = jnp.maximum(a, b), jnp.minimum(a, b)
        hi_cd, lo_cd = jnp.maximum(c, d), jnp.minimum(c, d)
        top1 = jnp.maximum(hi_ab, hi_cd)
        top2 = jnp.maximum(jnp.maximum(lo_ab, lo_cd), jnp.minimum(hi_ab, hi_cd))
        gs.append(top1 + top2)
    best = jnp.zeros_like(gs[0])
    best_v = gs[0]
    for g in range(1, N_GROUPS):
        upd = gs[g] > best_v
        best = jnp.where(upd, float(g), best)
        best_v = jnp.where(upd, gs[g], best_v)
    masked = [jnp.where(best == float(e // EXP_PER_GROUP), se[e], NEG) for e in range(N_EXP)]
    i1 = jnp.zeros_like(best)
    v1 = masked[0]
    s1 = sc[0]
    for e in range(1, N_EXP):
        upd = masked[e] > v1
        i1 = jnp.where(upd, float(e), i1)
        v1 = jnp.where(upd, masked[e], v1)
        s1 = jnp.where(upd, sc[e], s1)
    i2 = jnp.zeros_like(best)
    v2 = jnp.full_like(v1, -jnp.inf)
    s2 = jnp.zeros_like(s1)
    for e in range(N_EXP):
        upd = jnp.logical_and(masked[e] > v2, i1 != float(e))
        i2 = jnp.where(upd, float(e), i2)
        v2 = jnp.where(upd, masked[e], v2)
        s2 = jnp.where(upd, sc[e], s2)
    tot = s1 + s2
    route_ref[0:1, :] = i1
    route_ref[1:2, :] = i2
    route_ref[2:3, :] = s1 / tot
    route_ref[3:4, :] = s2 / tot
    route_ref[4:8, :] = jnp.zeros((4, TM), F32)


def _finish_mixer(y, x_ref, mod_ref, g_ref, rwh_ref, rwl_ref, rb_ref, xn_ref, h2_ref, route_ref):
    xn = x_ref[...] + mod_ref[:, 2 * D:3 * D] * y
    xn_ref[...] = xn
    h2 = _modulate(xn, g_ref[...], mod_ref[:, 3 * D:4 * D], mod_ref[:, 4 * D:5 * D])
    h2_ref[...] = h2
    _route(h2, rwh_ref, rwl_ref, rb_ref, route_ref)


def _ab_out_kernel(hf_ref, hb_ref, gate_ref, att_ref, x_ref, mod_ref, g_ref, wo_ref, rwh_ref, rwl_ref, rb_ref,
                   xn_ref, h2_ref, route_ref):
    rnn = ((hf_ref[...] + hb_ref[...]) * gate_ref[...].astype(F32)).astype(BF16)
    y = (jnp.dot(rnn, wo_ref[0:D_RNN, :], preferred_element_type=F32)
         + jnp.dot(att_ref[...], wo_ref[D_RNN:, :], preferred_element_type=F32))
    _finish_mixer(y, x_ref, mod_ref, g_ref, rwh_ref, rwl_ref, rb_ref, xn_ref, h2_ref, route_ref)


def _na_out_kernel(att_ref, x_ref, mod_ref, g_ref, wo_ref, rwh_ref, rwl_ref, rb_ref, xn_ref, h2_ref, route_ref):
    y = jnp.dot(att_ref[...], wo_ref[...], preferred_element_type=F32)
    _finish_mixer(y, x_ref, mod_ref, g_ref, rwh_ref, rwl_ref, rb_ref, xn_ref, h2_ref, route_ref)


def _mixer_out(kernel, acts, x_all, mod, g, wo, rw_hi, rw_lo, rb, off):
    nb, t, _ = x_all.shape
    nt = t // TM - off
    row = lambda b, i: (b, i + off, 0)
    out_row = lambda b, i: (b, i, 0)
    const = lambda b, i: (0, 0)
    act_specs = [pl.BlockSpec((None, TM, a.shape[2]), row if a.shape[1] == t else out_row) for a in acts]
    return pl.pallas_call(
        kernel,
        grid=(nb, nt),
        in_specs=act_specs + [
            pl.BlockSpec((None, TM, D), row),
            _mod_spec(nb, off),
            _resident((1, D), const),
            _resident(wo.shape, const),
            _resident((N_EXP, D), const),
            _resident((N_EXP, D), const),
            _resident((N_EXP, 1), const),
        ],
        out_specs=[
            pl.BlockSpec((None, TM, D), out_row),
            pl.BlockSpec((None, TM, D), out_row),
            pl.BlockSpec((None, 8, TM), lambda b, i: (b, 0, i)),
        ],
        out_shape=[
            jax.ShapeDtypeStruct((nb, nt * TM, D), F32),
            jax.ShapeDtypeStruct((nb, nt * TM, D), F32),
            jax.ShapeDtypeStruct((nb, 8, nt * TM), F32),
        ],
        compiler_params=_cparams(("arbitrary", "arbitrary")),
        name="mixer_out",
    )(*acts, x_all, mod, g, wo, rw_hi, rw_lo, rb)


def _na_in_kernel(x_ref, g_ref, mod_ref, w_ref, hg_ref, o_ref):
    p = pl.program_id(0)
    h = _modulate(x_ref[...], g_ref[...], mod_ref[:, 0:D], mod_ref[:, D:2 * D]).astype(BF16)
    u = jnp.dot(h, w_ref[...], preferred_element_type=F32)

    @pl.when(p < 2)
    def _():
        hg = hg_ref[...]
        for hh in range(NA_H):
            uh = u[:, hh * NA_DH:(hh + 1) * NA_DH]
            o_ref[:, hh * NA_DH:(hh + 1) * NA_DH] = _rms(uh, hg, NA_DH).astype(BF16)

    @pl.when(p == 2)
    def _():
        o_ref[...] = u.astype(BF16)


def _na_in(x_all, g, mod, w_qkv, hg):
    nb, t, _ = x_all.shape
    nt = t // TM
    return pl.pallas_call(
        _na_in_kernel,
        grid=(3, nb, nt),
        in_specs=[
            pl.BlockSpec((None, TM, D), lambda p, b, i: (b, i, 0)),
            _resident((1, D), lambda p, b, i: (0, 0)),
            pl.BlockSpec((None, 1, ADA * D), lambda p, b, i: (jnp.where(i == 0, nb, b), 0, 0)),
            pl.BlockSpec((D, D), lambda p, b, i: (0, p)),
            pl.BlockSpec((None, 1, NA_DH), lambda p, b, i: (p, 0, 0)),
        ],
        out_specs=pl.BlockSpec((None, None, TM, D), lambda p, b, i: (p, b, i, 0)),
        out_shape=jax.ShapeDtypeStruct((3, nb, t, D), BF16),
        compiler_params=_cparams(("arbitrary", "arbitrary", "arbitrary")),
        name="na_in",
    )(x_all, g, mod, w_qkv, hg)


def _na_key_base(rb, rows):
    return jnp.clip(rb * NA_QR - NA_ROWS // 2, 0, rows - NA_KR)


NA_HPS = 4


def _na_attn_kernel(var_ref, q_ref, k_ref, v_ref, bias_ref, o_ref, *, rows):
    rb = pl.program_id(2)
    start = pl.multiple_of(TM + _na_key_base(rb, rows) * GRID_W, GRID_W)
    nk = NA_KR * GRID_W
    nt_dims = (((1,), (1,)), ((), ()))
    for hh in range(NA_HPS):
        hs = slice(hh * NA_DH, (hh + 1) * NA_DH)
        q = q_ref[:, hs]
        s_c = lax.dot_general(q, k_ref[0:TM, hs], nt_dims, preferred_element_type=F32)
        s_l = lax.dot_general(q, k_ref[pl.ds(start, nk), hs], nt_dims, preferred_element_type=F32) + bias_ref[hh]
        m = jnp.maximum(jnp.max(s_c, axis=-1, keepdims=True), jnp.max(s_l, axis=-1, keepdims=True))
        p_c = jnp.exp(s_c - m)
        p_l = jnp.exp(s_l - m)
        l = jnp.sum(p_c, axis=-1, keepdims=True) + jnp.sum(p_l, axis=-1, keepdims=True)
        o = (jnp.dot(p_l.astype(BF16), v_ref[pl.ds(start, nk), hs], preferred_element_type=F32)
             + jnp.dot(p_c.astype(BF16), v_ref[0:TM, hs], preferred_element_type=F32))
        o_ref[:, hs] = (o / l).astype(BF16)


def _na_attn(qkv, bias, variant):
    _, nb, t, _ = qkv.shape
    rows = (t - TM) // GRID_W
    nrb = rows // NA_QR
    nq = NA_QR * GRID_W
    assert nq == TM
    return pl.pallas_call(
        functools.partial(_na_attn_kernel, rows=rows),
        grid_spec=pltpu.PrefetchScalarGridSpec(
            num_scalar_prefetch=1,
            grid=(NA_H // NA_HPS, nb, nrb),
            in_specs=[
                pl.BlockSpec((None, None, nq, NA_HPS * NA_DH), lambda h, b, r, var: (0, b, r + 1, h)),
                pl.BlockSpec((None, None, t, NA_HPS * NA_DH), lambda h, b, r, var: (1, b, 0, h)),
                pl.BlockSpec((None, None, t, NA_HPS * NA_DH), lambda h, b, r, var: (2, b, 0, h)),
                pl.BlockSpec((NA_HPS, None, nq, NA_KR * GRID_W), lambda h, b, r, var: (h, var[r], 0, 0)),
            ],
            out_specs=pl.BlockSpec((None, nq, NA_HPS * NA_DH), lambda h, b, r, var: (b, r, h)),
        ),
        out_shape=jax.ShapeDtypeStruct((nb, t - TM, D), BF16),
        compiler_params=_cparams(("arbitrary", "arbitrary", "arbitrary")),
        name="na_attn",
    )(variant, qkv, qkv, qkv, bias)


def _na_bias_tables(rpb, rows):
    nrb = rows // NA_QR
    rb = np.arange(nrb)
    kb = np.clip(rb * NA_QR - NA_ROWS // 2, 0, rows - NA_KR)
    r_q = rb[:, None] * NA_QR + np.arange(NA_QR)[None, :]
    r0_q = np.clip(r_q - NA_ROWS // 2, 0, rows - NA_ROWS)
    sig = np.concatenate([r_q - kb[:, None], r0_q - kb[:, None]], axis=1)
    uniq, variant = np.unique(sig, axis=0, return_inverse=True)
    variant = np.asarray(variant).reshape(-1)
    kr = np.arange(NA_KR)[None, :]
    cols = np.arange(GRID_W)
    col_start = np.clip(cols - NA_COLS // 2, 0, GRID_W - NA_COLS)
    kc = cols[None, :]
    col_ok = (kc >= col_start[:, None]) & (kc < col_start[:, None] + NA_COLS)
    nh, ndr, ndc = rpb.shape
    per = 2 * GRID_W + 1
    p = jnp.concatenate([rpb[:, :, NA_COLS - 1:], jnp.zeros((nh, ndr, per - ndc), F32), rpb[:, :, :NA_COLS - 1]], axis=2)
    toep = jnp.tile(p, (1, 1, GRID_W))[:, :, :GRID_W * (per - 1)].reshape(nh, ndr, GRID_W, per - 1)[:, :, :, :GRID_W]
    toep = jnp.where(col_ok[None, None], toep, NEG)
    masked = jnp.full((nh, GRID_W, GRID_W), NEG, F32)
    tabs = []
    for u in uniq:
        r_abs = u[:NA_QR][:, None]
        r0 = u[NA_QR:][:, None]
        row_ok = (kr >= r0) & (kr < r0 + NA_ROWS)
        drow = kr - r_abs + NA_ROWS - 1
        blocks = [jnp.stack([toep[:, int(drow[a, b])] if row_ok[a, b] else masked for b in range(NA_KR)], axis=2)
                  for a in range(NA_QR)]
        tabs.append(jnp.stack(blocks, axis=1).reshape(nh, NA_QR * GRID_W, NA_KR * GRID_W))
    return jnp.stack(tabs, axis=1), jnp.asarray(variant, I32)


def _dispatch_kernel(pos_ref, padpos_ref, h2_ref, xs_ref, zero_ref, sem, *, nt):
    tile = pl.program_id(0) * nt + pl.program_id(1)

    def pad_copy(q):
        return pltpu.make_async_copy(zero_ref.at[pl.ds(0, 1), :], xs_ref.at[pl.ds(padpos_ref[q], 1), :], sem)

    @pl.when(tile == 0)
    def _():
        zero_ref[...] = jnp.zeros_like(zero_ref)

        def start(q, c):
            pad_copy(q).start()
            return c

        def wait(q, c):
            pad_copy(q).wait()
            return c

        lax.fori_loop(0, N_EXP * TM, start, 0, unroll=16)
        lax.fori_loop(0, N_EXP * TM, wait, 0, unroll=16)

    base = tile * (2 * TM)

    def row_copy(j, kk):
        return pltpu.make_async_copy(h2_ref.at[pl.ds(j, 1), :], xs_ref.at[pl.ds(pos_ref[base + 2 * j + kk], 1), :], sem)

    for j in range(TM):
        row_copy(j, 0).start()
        row_copy(j, 1).start()
    for j in range(TM):
        row_copy(j, 0).wait()
        row_copy(j, 1).wait()


def _dispatch(h2, pos, padpos, n_rows):
    nb, t, _ = h2.shape
    nt = t // TM
    return pl.pallas_call(
        functools.partial(_dispatch_kernel, nt=nt),
        grid_spec=pltpu.PrefetchScalarGridSpec(
            num_scalar_prefetch=2,
            grid=(nb, nt),
            in_specs=[pl.BlockSpec((TM, D), lambda b, i, *_: (b * nt + i, 0))],
            out_specs=pl.BlockSpec(memory_space=pl.ANY),
            scratch_shapes=[pltpu.VMEM((8, D), F32), pltpu.SemaphoreType.DMA],
        ),
        out_shape=jax.ShapeDtypeStruct((n_rows, D), F32),
        compiler_params=_cparams(("arbitrary", "arbitrary")),
        name="moe_dispatch",
    )(pos, padpos, h2.reshape(nb * t, D))


def _swiglu_tile(x_bf16, wg, wu, wd):
    gate = jnp.dot(x_bf16, wg, preferred_element_type=F32)
    up = jnp.dot(x_bf16, wu, preferred_element_type=F32)
    hmid = (_silu(gate) * up).astype(BF16)
    return jnp.dot(hmid, wd, preferred_element_type=F32)


def _expert_kernel(te_ref, tb_ref, nu_ref, x_ref, wg_ref, wu_ref, wd_ref, y_ref):
    g = pl.program_id(0)

    @pl.when(g < nu_ref[0])
    def _():
        y_ref[...] = _swiglu_tile(x_ref[...].astype(BF16), wg_ref[...], wu_ref[...], wd_ref[...])

    @pl.when(g >= nu_ref[0])
    def _():
        y_ref[...] = jnp.zeros_like(y_ref)


def _experts(xs, tile_expert, tile_block, n_used, wg, wu, wd, layer):
    n_rows = xs.shape[0]
    n_tiles = n_rows // TM
    return pl.pallas_call(
        _expert_kernel,
        grid_spec=pltpu.PrefetchScalarGridSpec(
            num_scalar_prefetch=3,
            grid=(n_tiles,),
            in_specs=[
                pl.BlockSpec((TM, D), lambda g, te, tb, nu: (tb[g], 0)),
                pl.BlockSpec((None, None, D, D_EXP), lambda g, te, tb, nu: (layer, te[g], 0, 0)),
                pl.BlockSpec((None, None, D, D_EXP), lambda g, te, tb, nu: (layer, te[g], 0, 0)),
                pl.BlockSpec((None, None, D_EXP, D), lambda g, te, tb, nu: (layer, te[g], 0, 0)),
            ],
            out_specs=pl.BlockSpec((TM, D), lambda g, te, tb, nu: (g, 0)),
        ),
        out_shape=jax.ShapeDtypeStruct((n_rows, D), F32),
        compiler_params=_cparams(("arbitrary",)),
        name="moe_experts",
    )(tile_expert, tile_block, n_used, xs, wg, wu, wd)


def _combine_kernel(pos_ref, h2_ref, xn_ref, mod_ref, wt_ref, wsg_ref, wsu_ref, wsd_ref, ys_ref, o_ref, ybuf, sem, *, nt):
    tile = pl.program_id(0) * nt + pl.program_id(1)
    base = tile * (2 * TM)

    def row_copy(j, kk):
        return pltpu.make_async_copy(ys_ref.at[pl.ds(pos_ref[base + 2 * j + kk], 1), :], ybuf.at[kk, pl.ds(j, 1), :], sem)

    for j in range(TM):
        row_copy(j, 0).start()
        row_copy(j, 1).start()
    shared = _swiglu_tile(h2_ref[...].astype(BF16), wsg_ref[...], wsu_ref[...], wsd_ref[...])
    for j in range(TM):
        row_copy(j, 0).wait()
        row_copy(j, 1).wait()
    wt = wt_ref[...]
    y = shared + wt[:, 0:1] * ybuf[0] + wt[:, 1:2] * ybuf[1]
    o_ref[...] = xn_ref[...] + mod_ref[:, 5 * D:6 * D] * y


def _combine(pos, h2, xn, mod, wt, wsg, wsu, wsd, ys, off, layer):
    nb, t, _ = h2.shape
    nt = t // TM
    row = lambda b, i, *_: (b, i, 0)
    const = lambda b, i, *_: (layer, 0, 0)
    return pl.pallas_call(
        functools.partial(_combine_kernel, nt=nt),
        grid_spec=pltpu.PrefetchScalarGridSpec(
            num_scalar_prefetch=1,
            grid=(nb, nt),
            in_specs=[
                pl.BlockSpec((None, TM, D), row),
                pl.BlockSpec((None, TM, D), row),
                pl.BlockSpec((None, 1, ADA * D), lambda b, i, *_: (jnp.where(i + off == 0, nb, b), 0, 0)),
                pl.BlockSpec((None, TM, 8), row),
                _resident((None, D, D_EXP), const),
                _resident((None, D, D_EXP), const),
                _resident((None, D_EXP, D), const),
                pl.BlockSpec(memory_space=pl.ANY),
            ],
            out_specs=pl.BlockSpec((None, TM, D), row),
            scratch_shapes=[pltpu.VMEM((2, TM, D), F32), pltpu.SemaphoreType.DMA],
        ),
        out_shape=jax.ShapeDtypeStruct((nb, t, D), F32),
        compiler_params=_cparams(("arbitrary", "arbitrary")),
        name="moe_combine",
    )(pos, h2, xn, mod, wt, wsg, wsu, wsd, ys)


def _moe_plan(route):
    nb, _, tp = route.shape
    n_tok = nb * tp
    idx = route[:, 0:2, :].astype(I32).transpose(0, 2, 1).reshape(n_tok * 2)
    wt = jnp.pad(route[:, 2:4, :].transpose(0, 2, 1), ((0, 0), (0, 0), (0, 6)))
    onehot = (idx[:, None] == jnp.arange(N_EXP, dtype=I32)[None, :]).astype(I32)
    csum = jnp.cumsum(onehot, axis=0)
    counts = csum[-1]
    rank = jnp.sum((csum - onehot) * onehot, axis=1)
    padded = ((counts + TM - 1) // TM) * TM
    ends = jnp.cumsum(padded)
    starts = ends - padded
    pos = (jnp.sum(onehot * starts[None, :], axis=1) + rank).astype(I32)
    n_tiles = (2 * n_tok) // TM + N_EXP
    n_used = (ends[-1] // TM).astype(I32)
    tile_block = jnp.minimum(jnp.arange(n_tiles, dtype=I32), n_used - 1)
    tile_expert = jnp.sum((tile_block[:, None] * TM >= ends[None, :]).astype(I32), axis=1).astype(I32)
    npads = padded - counts
    pad_ends = jnp.cumsum(npads)
    pad_starts = pad_ends - npads
    qidx = jnp.arange(N_EXP * TM, dtype=I32)
    pe = jnp.minimum(jnp.sum((qidx[:, None] >= pad_ends[None, :]).astype(I32), axis=1), N_EXP - 1)
    in_seg = starts[pe] + counts[pe] + (qidx - pad_starts[pe])
    tail = ends[-1] + (qidx - pad_ends[-1])
    padpos = jnp.where(qidx < pad_ends[-1], in_seg, tail).astype(I32)
    return pos, padpos, tile_expert, tile_block, n_used.reshape(1), wt, n_tiles * TM


def _moe(h2, xn, route, mod, w, off, layer):
    pos, padpos, tile_expert, tile_block, n_used, wt, n_rows = _moe_plan(route)
    xs = _dispatch(h2, pos, padpos, n_rows)
    ys = _experts(xs, tile_expert, tile_block, n_used, w["wg"], w["wu"], w["wd"], layer)
    return _combine(pos, h2, xn, mod, wt, w["wsg"], w["wsu"], w["wsd"], ys, off, layer)


def _rope_tables(t, s):
    half = MLA_ROPE // 2
    inv_freq = ROPE_THETA ** (-jnp.arange(0, half, 2, dtype=F32) / half)
    tt = jnp.arange(t - s)
    ang_row = (tt // GRID_W).astype(F32)[:, None] * inv_freq
    ang_col = (tt % GRID_W).astype(F32)[:, None] * inv_freq
    ang = jnp.concatenate([ang_row] * 2 + [ang_col] * 2, axis=1)
    cos = jnp.cos(ang)
    sin = jnp.sin(ang)
    sign = np.where((np.arange(MLA_ROPE) % 32) < 16, -1.0, 1.0).astype(np.float32)
    sin = sin * sign[None, :]
    cos = jnp.concatenate([jnp.ones((s, MLA_ROPE), F32), cos], axis=0)
    sin = jnp.concatenate([jnp.zeros((s, MLA_ROPE), F32), sin], axis=0)
    cos = jnp.pad(cos, ((0, 0), (0, LANE - MLA_ROPE)), constant_values=1.0)
    sin = jnp.pad(sin, ((0, 0), (0, LANE - MLA_ROPE)))
    return cos.astype(F32), sin.astype(F32)


def _ab_weights(j, t, s, ab_w_in, lru_conv_w, lru_conv_b, lru_w_a, lru_b_a, lru_w_i, lru_b_i, lru_lambda,
                mla_q_norm_g, mla_w_uq, mla_kv_norm_g, mla_w_ukv, mla_q_head_g, mla_k_head_g):
    w_in = jnp.pad(ab_w_in[j], ((0, 0), (0, AB_IN_PAD - AB_IN))).astype(BF16)
    uq = mla_w_uq[j].reshape(Q_RANK, MLA_H, MLA_QK)
    uq_n = uq[:, :, :MLA_NOPE].reshape(Q_RANK, MLA_H * LANE)
    uq_r = jnp.pad(uq[:, :, MLA_NOPE:], ((0, 0), (0, 0), (0, LANE - MLA_ROPE))).reshape(Q_RANK, MLA_H * LANE)
    ukv = mla_w_ukv[j].reshape(KV_RANK, MLA_H, MLA_NOPE + MLA_V)
    ukv_k = ukv[:, :, :MLA_NOPE].reshape(KV_RANK, MLA_H * LANE)
    ukv_v = ukv[:, :, MLA_NOPE:].reshape(KV_RANK, MLA_H * MLA_V)
    scale = MLA_QK ** -0.5
    qg = mla_q_head_g[j] * scale
    kg = mla_k_head_g[j]
    pad_r = lambda g: jnp.pad(g[MLA_NOPE:], (0, LANE - MLA_ROPE))
    hg = jnp.stack([qg[:MLA_NOPE], pad_r(qg), kg[:MLA_NOPE], pad_r(kg)], axis=0)
    cos, sin = _rope_tables(t, s)
    return dict(
        w_in=w_in, qng=mla_q_norm_g[j][None], kvng=mla_kv_norm_g[j][None],
        w_uq=jnp.concatenate([uq_n, uq_r], axis=1).astype(BF16),
        w_ukv=jnp.concatenate([ukv_k, ukv_v], axis=1).astype(BF16),
        hg=hg, cos=cos, sin=sin,
        conv_w=lru_conv_w[j], conv_b=lru_conv_b[j][None],
        w_a=lru_w_a[j].astype(BF16), b_a=lru_b_a[j][:, None, :],
        w_i=lru_w_i[j].astype(BF16), b_i=lru_b_i[j][:, None, :], lam=lru_lambda[j][:, None, :],
    )


def kernel(x, c, ctx, c_ctx, ada_w, ada_b, norm_mix_g, norm_ffn_g, ab_w_in, ab_w_out, lru_conv_w, lru_conv_b, lru_w_a, lru_b_a, lru_w_i, lru_b_i, lru_lambda, mla_q_norm_g, mla_w_uq, mla_kv_norm_g, mla_w_ukv, mla_q_head_g, mla_k_head_g, na_w_qkv, na_w_out, na_q_head_g, na_k_head_g, na_rpb, router_w, router_b, moe_w_gate, moe_w_up, moe_w_down, moe_ws_gate, moe_ws_up, moe_ws_down):
    nb, seq, _ = x.shape
    s = ctx.shape[1]
    depth = ada_w.shape[0]
    assert s == TM and seq % TM == 0 and nb < 8
    t = s + seq
    rows = seq // GRID_W

    cs = jnp.zeros((8, D), F32).at[:nb].set(c).at[nb].set(c_ctx)
    mod_all = _adaln(cs, ada_w, ada_b).reshape(depth, 8, 1, ADA * D)

    rw_t = router_w.T
    rw_hi = rw_t.astype(BF16)
    rw_lo = (rw_t - rw_hi.astype(F32)).astype(BF16)
    rb = router_b.reshape(N_EXP, 1)

    moe_w = dict(wg=moe_w_gate.astype(BF16), wu=moe_w_up.astype(BF16), wd=moe_w_down.astype(BF16),
                 wsg=moe_ws_gate.astype(BF16), wsu=moe_ws_up.astype(BF16), wsd=moe_ws_down.astype(BF16))

    x_all = jnp.concatenate([ctx, x], axis=1)
    for i in range(depth):
        last = i == depth - 1
        off = 1 if last else 0
        j = i // 2
        mod = mod_all[i]
        g_mix = norm_mix_g[i][None]
        g_ffn = norm_ffn_g[i][None]
        if i % 2 == 0:
            w = _ab_weights(j, t, s, ab_w_in, lru_conv_w, lru_conv_b, lru_w_a, lru_b_a, lru_w_i, lru_b_i, lru_lambda,
                            mla_q_norm_g, mla_w_uq, mla_kv_norm_g, mla_w_ukv, mla_q_head_g, mla_k_head_g)
            ux, gate, q, k, v = _ab_in(x_all, g_mix, mod, w)
            h_f, h_b = _lru(ux, w)
            att = _mla_attn(q, k, v)
            xn, h2, route = _mixer_out(_ab_out_kernel, (h_f, h_b, gate, att),
                                       x_all, mod, g_ffn, ab_w_out[j].astype(BF16), rw_hi, rw_lo, rb, off)
        else:
            scale = NA_DH ** -0.5
            hg = jnp.stack([na_q_head_g[j] * scale, na_k_head_g[j], jnp.ones((NA_DH,), F32)], axis=0)[:, None, :]
            qkv = _na_in(x_all, g_mix, mod, na_w_qkv[j].astype(BF16), hg)
            bias, variant = _na_bias_tables(na_rpb[j], rows)
            att = _na_attn(qkv, bias, variant)
            assert last, "the neighbourhood layer has no context-output path"
            xn, h2, route = _mixer_out(_na_out_kernel, (att,), x_all, mod, g_ffn, na_w_out[j].astype(BF16),
                                       rw_hi, rw_lo, rb, off)
        x_all = _moe(h2, xn, route, mod, moe_w, off, i)
    return x_all
```

```python
import functools

import numpy as np
import jax
import jax.numpy as jnp
from jax import lax
from jax.experimental import pallas as pl
from jax.experimental.pallas import tpu as pltpu

F32 = jnp.float32
BF16 = jnp.bfloat16
I32 = jnp.int32

D = 2048
ADA = 6
EPS = 1e-6
NEG = -1e30
LOG2E = 1.4426950408889634
GRID_W = 64
D_RNN = 1024
LRU_BLOCKS = 8
LRU_BW = 128
LRU_C = 8.0
MLA_H = 8
MLA_NOPE = 128
MLA_ROPE = 64
MLA_QK = 192
MLA_V = 128
Q_RANK = 512
KV_RANK = 512
ROPE_THETA = 10000.0
AB_IN = 3136
AB_IN_PAD = 3200
NA_H = 16
NA_DH = 128
NA_ROWS = 8
NA_COLS = 16
N_EXP = 16
N_GROUPS = 4
EXP_PER_GROUP = 4
D_EXP = 1408

LANE = 128
SUBLANES = 8
TM = 256
VMEM_LIMIT = 56 * 1024 * 1024

NA_QR = 4
NA_KR = 12


def _cparams(sem, vmem=VMEM_LIMIT):
    return pltpu.CompilerParams(dimension_semantics=sem, vmem_limit_bytes=vmem)


def _resident(shape, index_map):
    return pl.BlockSpec(shape, index_map, pipeline_mode=pl.Buffered(1))


def _rms(x, g, n):
    ms = jnp.sum(x * x, axis=-1, keepdims=True) * (1.0 / n)
    return (x * lax.rsqrt(ms + EPS)) * g


def _modulate(x, g, shift, scale):
    return _rms(x, g, D) * (1.0 + scale) + shift


def _gelu_tanh(x):
    c = 0.7978845608028654
    return x * (0.5 * (1.0 + jnp.tanh(c * (x + 0.044715 * (x * x * x)))))


def _silu(x):
    return x * jax.nn.sigmoid(x)


ADA_TN = 1536


def _adaln_kernel(cs_ref, w_ref, b_ref, o_ref):
    a = _silu(cs_ref[...]).astype(BF16)
    o_ref[...] = jnp.dot(a, w_ref[...].astype(BF16), preferred_element_type=F32) + b_ref[...]


def _adaln(cs, ada_w, ada_b):
    depth = ada_w.shape[0]
    n = ADA * D
    return pl.pallas_call(
        _adaln_kernel,
        grid=(depth, n // ADA_TN),
        in_specs=[
            pl.BlockSpec((8, D), lambda l, j: (0, 0)),
            pl.BlockSpec((None, D, ADA_TN), lambda l, j: (l, 0, j)),
            pl.BlockSpec((None, 1, ADA_TN), lambda l, j: (l, 0, j)),
        ],
        out_specs=pl.BlockSpec((None, 8, ADA_TN), lambda l, j: (l, 0, j)),
        out_shape=jax.ShapeDtypeStruct((depth, 8, n), F32),
        compiler_params=_cparams(("arbitrary", "arbitrary")),
        name="adaln",
    )(cs, ada_w, ada_b.reshape(depth, 1, n))


def _mod_spec(nb, off):
    return pl.BlockSpec((None, 1, ADA * D), lambda b, i: (jnp.where(i + off == 0, nb, b), 0, 0))


def _ab_in_kernel(x_ref, g_ref, mod_ref, win_ref, qng_ref, kvng_ref, wuq_ref, wukv_ref, hg_ref, cos_ref, sin_ref,
                  ux_ref, gate_ref, q_ref, k_ref, v_ref):
    h = _modulate(x_ref[...], g_ref[...], mod_ref[:, 0:D], mod_ref[:, D:2 * D]).astype(BF16)
    u = jnp.dot(h, win_ref[...], preferred_element_type=F32)
    ux_ref[...] = u[:, 0:D_RNN].astype(BF16)
    gate_ref[...] = _gelu_tanh(u[:, D_RNN:2 * D_RNN]).astype(BF16)
    o2 = 2 * D_RNN
    cq = _rms(u[:, o2:o2 + Q_RANK], qng_ref[...], Q_RANK).astype(BF16)
    ckv = _rms(u[:, o2 + Q_RANK:o2 + Q_RANK + KV_RANK], kvng_ref[...], KV_RANK).astype(BF16)
    qf = jnp.dot(cq, wuq_ref[...], preferred_element_type=F32)
    kvf = jnp.dot(ckv, wukv_ref[...], preferred_element_type=F32)
    kr = u[:, o2 + Q_RANK + KV_RANK:AB_IN_PAD]
    cos = cos_ref[...]
    sin = sin_ref[...]
    lane = lax.broadcasted_iota(I32, (TM, LANE), 1)
    first_half = (lane % 32) < 16

    def rope(z):
        partner = jnp.where(first_half, pltpu.roll(z, LANE - 16, 1), pltpu.roll(z, 16, 1))
        return z * cos + partner * sin

    hg = hg_ref[...]
    kr_ss = jnp.sum(kr * kr, axis=-1, keepdims=True)
    nh = MLA_H * MLA_NOPE
    for hh in range(MLA_H):
        qn = qf[:, hh * LANE:(hh + 1) * LANE]
        qr = qf[:, nh + hh * LANE:nh + (hh + 1) * LANE]
        ss = jnp.sum(qn * qn, axis=-1, keepdims=True) + jnp.sum(qr * qr, axis=-1, keepdims=True)
        inv = lax.rsqrt(ss * (1.0 / MLA_QK) + EPS)
        q_ref[:, hh * 256:hh * 256 + LANE] = ((qn * inv) * hg[0:1]).astype(BF16)
        q_ref[:, hh * 256 + LANE:(hh + 1) * 256] = rope((qr * inv) * hg[1:2]).astype(BF16)
        kn = kvf[:, hh * LANE:(hh + 1) * LANE]
        ss = jnp.sum(kn * kn, axis=-1, keepdims=True) + kr_ss
        inv = lax.rsqrt(ss * (1.0 / MLA_QK) + EPS)
        k_ref[:, hh * 256:hh * 256 + LANE] = ((kn * inv) * hg[2:3]).astype(BF16)
        k_ref[:, hh * 256 + LANE:(hh + 1) * 256] = rope((kr * inv) * hg[3:4]).astype(BF16)
    v_ref[...] = kvf[:, nh:].astype(BF16)


def _ab_in(x_all, g, mod, w):
    nb, t, _ = x_all.shape
    nt = t // TM
    row = lambda b, i: (b, i, 0)
    const = lambda b, i: (0, 0)
    outs = [
        jax.ShapeDtypeStruct((nb, t, D_RNN), BF16),
        jax.ShapeDtypeStruct((nb, t, D_RNN), BF16),
        jax.ShapeDtypeStruct((nb, t, MLA_H * 256), BF16),
        jax.ShapeDtypeStruct((nb, t, MLA_H * 256), BF16),
        jax.ShapeDtypeStruct((nb, t, MLA_H * MLA_V), BF16),
    ]
    return pl.pallas_call(
        _ab_in_kernel,
        grid=(nb, nt),
        in_specs=[
            pl.BlockSpec((None, TM, D), row),
            _resident((1, D), const),
            _mod_spec(nb, 0),
            _resident((D, AB_IN_PAD), const),
            _resident((1, Q_RANK), const),
            _resident((1, KV_RANK), const),
            _resident((Q_RANK, 2 * MLA_H * LANE), const),
            _resident((KV_RANK, 2 * MLA_H * LANE), const),
            _resident((4, LANE), const),
            pl.BlockSpec((TM, LANE), lambda b, i: (i, 0)),
            pl.BlockSpec((TM, LANE), lambda b, i: (i, 0)),
        ],
        out_specs=[
            pl.BlockSpec((None, TM, D_RNN), row),
            pl.BlockSpec((None, TM, D_RNN), row),
            pl.BlockSpec((None, TM, MLA_H * 256), row),
            pl.BlockSpec((None, TM, MLA_H * 256), row),
            pl.BlockSpec((None, TM, MLA_H * MLA_V), row),
        ],
        out_shape=outs,
        compiler_params=_cparams(("arbitrary", "arbitrary")),
        name="ab_in",
    )(x_all, g, mod, w["w_in"], w["qng"], w["kvng"], w["w_uq"], w["w_ukv"], w["hg"], w["cos"], w["sin"])


def _lru_dir(prev_ref, cur_ref, next_ref, c, n_chunks, d, reverse,
             cw_ref, cb_ref, wa_ref, ba_ref, wi_ref, bi_ref, lam_ref, carry_ref, out_ref):
    xc = cur_ref[...].astype(F32)
    row = lax.broadcasted_iota(I32, (TM, 1), 0)
    left_ok = jnp.where(c >= 2, 1.0, 0.0)
    right_ok = jnp.where(jnp.logical_and(c >= 1, c <= n_chunks - 2), 1.0, 0.0)
    p0 = prev_ref[TM - 2:TM - 1, :].astype(F32) * left_ok
    p1 = prev_ref[TM - 1:TM, :].astype(F32) * left_ok
    n0 = next_ref[0:1, :].astype(F32) * right_ok
    x_m2 = jnp.where(row >= 2, pltpu.roll(xc, 2, 0), jnp.where(row == 0, p0, p1))
    x_m1 = jnp.where(row >= 1, pltpu.roll(xc, 1, 0), p1)
    x_p1 = jnp.where(row <= TM - 2, pltpu.roll(xc, TM - 1, 0), n0)
    cw = cw_ref[...]
    xconv = x_m2 * cw[0:1] + x_m1 * cw[1:2] + xc * cw[2:3] + x_p1 * cw[3:4] + cb_ref[...]

    xb = xconv.astype(BF16)
    ra = []
    ri = []
    for n in range(LRU_BLOCKS):
        xs = xb[:, n * LRU_BW:(n + 1) * LRU_BW]
        ra.append(jnp.dot(xs, wa_ref[d, n], preferred_element_type=F32))
        ri.append(jnp.dot(xs, wi_ref[d, n], preferred_element_type=F32))
    r = jax.nn.sigmoid(jnp.concatenate(ra, axis=1) + ba_ref[d])
    gi = jax.nn.sigmoid(jnp.concatenate(ri, axis=1) + bi_ref[d])
    neg_lam = -lam_ref[d]
    softplus = jnp.maximum(neg_lam, 0.0) + jnp.log1p(jnp.exp(-jnp.abs(neg_lam)))
    log_a = (-LRU_C * r) * softplus
    a = jnp.exp(log_a)
    th = jnp.tanh(log_a)
    neg_expm1 = (-2.0 * th) / (1.0 - th)
    bb = jnp.sqrt(neg_expm1) * gi * xconv

    ng = TM // SUBLANES
    a3 = a.reshape(ng, SUBLANES, D_RNN)
    b3 = bb.reshape(ng, SUBLANES, D_RNN)
    sub = lax.broadcasted_iota(I32, (1, SUBLANES, 1), 1)
    k = 1
    while k < SUBLANES:
        if reverse:
            keep = sub < SUBLANES - k
            a_s = jnp.where(keep, pltpu.roll(a3, SUBLANES - k, 1), 1.0)
            b_s = jnp.where(keep, pltpu.roll(b3, SUBLANES - k, 1), 0.0)
        else:
            keep = sub >= k
            a_s = jnp.where(keep, pltpu.roll(a3, k, 1), 1.0)
            b_s = jnp.where(keep, pltpu.roll(b3, k, 1), 0.0)
        b3 = b3 + a3 * b_s
        a3 = a3 * a_s
        k *= 2
    hcar = carry_ref[d:d + 1, :]
    last = 0 if reverse else SUBLANES - 1
    for g in (range(ng - 1, -1, -1) if reverse else range(ng)):
        h = b3[g] + a3[g] * hcar
        out_ref[g * SUBLANES:(g + 1) * SUBLANES, :] = h
        hcar = h[last:last + 1, :]
    carry_ref[d:d + 1, :] = hcar


def _lru_kernel(pf_ref, cf_ref, nf_ref, pb_ref, cbk_ref, nb_ref, cw_ref, cb_ref, wa_ref, ba_ref, wi_ref, bi_ref, lam_ref,
                hf_ref, hb_ref, carry_ref, *, n_chunks):
    s = pl.program_id(1)

    @pl.when(s == 0)
    def _():
        carry_ref[...] = jnp.zeros_like(carry_ref)

    params = (cw_ref, cb_ref, wa_ref, ba_ref, wi_ref, bi_ref, lam_ref, carry_ref)
    _lru_dir(pf_ref, cf_ref, nf_ref, s, n_chunks, 0, False, *params, hf_ref)
    cb_idx = jnp.where(s == 0, 0, n_chunks - s)
    _lru_dir(pb_ref, cbk_ref, nb_ref, cb_idx, n_chunks, 1, True, *params, hb_ref)


def _lru(ux, w):
    nb, t, _ = ux.shape
    nt = t // TM
    fwd = lambda s: s
    bwd = lambda s: jnp.where(s == 0, 0, nt - s)
    prev = lambda c: jnp.maximum(c - 1, 0)
    nxt = lambda c: jnp.minimum(c + 1, nt - 1)
    blk = (None, TM, D_RNN)
    const2 = lambda b, s: (0, 0)
    const3 = lambda b, s: (0, 0, 0)
    const4 = lambda b, s: (0, 0, 0, 0)
    return pl.pallas_call(
        functools.partial(_lru_kernel, n_chunks=nt),
        grid=(nb, nt),
        in_specs=[
            pl.BlockSpec(blk, lambda b, s: (b, prev(fwd(s)), 0)),
            pl.BlockSpec(blk, lambda b, s: (b, fwd(s), 0)),
            pl.BlockSpec(blk, lambda b, s: (b, nxt(fwd(s)), 0)),
            pl.BlockSpec(blk, lambda b, s: (b, prev(bwd(s)), 0)),
            pl.BlockSpec(blk, lambda b, s: (b, bwd(s), 0)),
            pl.BlockSpec(blk, lambda b, s: (b, nxt(bwd(s)), 0)),
            _resident((4, D_RNN), const2),
            _resident((1, D_RNN), const2),
            _resident((2, LRU_BLOCKS, LRU_BW, LRU_BW), const4),
            _resident((2, 1, D_RNN), const3),
            _resident((2, LRU_BLOCKS, LRU_BW, LRU_BW), const4),
            _resident((2, 1, D_RNN), const3),
            _resident((2, 1, D_RNN), const3),
        ],
        out_specs=[
            pl.BlockSpec(blk, lambda b, s: (b, fwd(s), 0)),
            pl.BlockSpec(blk, lambda b, s: (b, bwd(s), 0)),
        ],
        out_shape=[jax.ShapeDtypeStruct((nb, t, D_RNN), F32)] * 2,
        scratch_shapes=[pltpu.VMEM((8, D_RNN), F32)],
        compiler_params=_cparams(("arbitrary", "arbitrary")),
        name="lru_scan",
    )(ux, ux, ux, ux, ux, ux, w["conv_w"], w["conv_b"], w["w_a"], w["b_a"], w["w_i"], w["b_i"], w["lam"])


def _softmax_av(q, k, v):
    s = lax.dot_general(q, k, (((1,), (1,)), ((), ())), preferred_element_type=F32)
    m = jnp.max(s, axis=-1, keepdims=True)
    p = jnp.exp2(s - m)
    l = jnp.sum(p, axis=-1, keepdims=True)
    o = jnp.dot(p.astype(BF16), v, preferred_element_type=F32)
    return o / l


MLA_HPS = 4


def _mla_attn_kernel(q_ref, k_ref, v_ref, o_ref):
    i = pl.program_id(2)

    def heads(n_keys):
        for hh in range(MLA_HPS):
            qs = slice(hh * 256, (hh + 1) * 256)
            vs = slice(hh * MLA_V, (hh + 1) * MLA_V)
            o_ref[:, vs] = _softmax_av(q_ref[:, qs], k_ref[0:n_keys, qs], v_ref[0:n_keys, vs]).astype(BF16)

    @pl.when(i == 0)
    def _():
        heads(TM)

    @pl.when(i > 0)
    def _():
        heads(k_ref.shape[0])


def _mla_attn(q, k, v):
    nb, t, _ = q.shape
    nt = t // TM
    return pl.pallas_call(
        _mla_attn_kernel,
        grid=(nb, MLA_H // MLA_HPS, nt),
        in_specs=[
            pl.BlockSpec((None, TM, MLA_HPS * 256), lambda b, h, i: (b, i, h)),
            pl.BlockSpec((None, t, MLA_HPS * 256), lambda b, h, i: (b, 0, h)),
            pl.BlockSpec((None, t, MLA_HPS * MLA_V), lambda b, h, i: (b, 0, h)),
        ],
        out_specs=pl.BlockSpec((None, TM, MLA_HPS * MLA_V), lambda b, h, i: (b, i, h)),
        out_shape=jax.ShapeDtypeStruct((nb, t, MLA_H * MLA_V), BF16),
        compiler_params=_cparams(("arbitrary", "arbitrary", "arbitrary")),
        name="mla_attn",
    )(q, k, v)


def _route(h2, rwh_ref, rwl_ref, rb_ref, route_ref):
    hi = h2.astype(BF16)
    lo = (h2 - hi.astype(F32)).astype(BF16)
    nt_dims = (((1,), (1,)), ((), ()))
    both = lax.dot_general(rwl_ref[...], hi, nt_dims, preferred_element_type=F32)
    logits = (both[0:N_EXP] + lax.dot_general(rwh_ref[...], lo, nt_dims, preferred_element_type=F32)
              + both[N_EXP:])
    scores = jax.nn.sigmoid(logits)
    sel = scores + rb_ref[...]
    sc = [scores[e:e + 1, :] for e in range(N_EXP)]
    se = [sel[e:e + 1, :] for e in range(N_EXP)]
    gs = []
    for g in range(N_GROUPS):
        a, b, c, d = se[4 * g:4 * g + 4]
        hi_ab, lo_ab = jnp.maximum(a, b), jnp.minimum(a, b)
        hi_cd, lo_cd = jnp.maximum(c, d), jnp.minimum(c, d)
        top1 = jnp.maximum(hi_ab, hi_cd)
        top2 = jnp.maximum(jnp.maximum(lo_ab, lo_cd), jnp.minimum(hi_ab, hi_cd))
        gs.append(top1 + top2)
    best = jnp.zeros_like(gs[0])
    best_v = gs[0]
    for g in range(1, N_GROUPS):
        upd = gs[g] > best_v
        best = jnp.where(upd, float(g), best)
        best_v = jnp.where(upd, gs[g], best_v)
    masked = [jnp.where(best == float(e // EXP_PER_GROUP), se[e], NEG) for e in range(N_EXP)]
    i1 = jnp.zeros_like(best)
    v1 = masked[0]
    s1 = sc[0]
    for e in range(1, N_EXP):
        upd = masked[e] > v1
        i1 = jnp.where(upd, float(e), i1)
        v1 = jnp.where(upd, masked[e], v1)
        s1 = jnp.where(upd, sc[e], s1)
    i2 = jnp.zeros_like(best)
    v2 = jnp.full_like(v1, -jnp.inf)
    s2 = jnp.zeros_like(s1)
    for e in range(N_EXP):
        upd = jnp.logical_and(masked[e] > v2, i1 != float(e))
        i2 = jnp.where(upd, float(e), i2)
        v2 = jnp.where(upd, masked[e], v2)
        s2 = jnp.where(upd, sc[e], s2)
    tot = s1 + s2
    route_ref[0:1, :] = i1
    route_ref[1:2, :] = i2
    route_ref[2:3, :] = s1 / tot
    route_ref[3:4, :] = s2 / tot
    route_ref[4:8, :] = jnp.zeros((4, TM), F32)


def _finish_mixer(y, x_ref, mod_ref, g_ref, rwh_ref, rwl_ref, rb_ref, xn_ref, h2_ref, route_ref):
    xn = x_ref[...] + mod_ref[:, 2 * D:3 * D] * y
    xn_ref[...] = xn
    h2 = _modulate(xn, g_ref[...], mod_ref[:, 3 * D:4 * D], mod_ref[:, 4 * D:5 * D])
    h2_ref[...] = h2
    _route(h2, rwh_ref, rwl_ref, rb_ref, route_ref)


def _ab_out_kernel(hf_ref, hb_ref, gate_ref, att_ref, x_ref, mod_ref, g_ref, wo_ref, rwh_ref, rwl_ref, rb_ref,
                   xn_ref, h2_ref, route_ref):
    rnn = ((hf_ref[...] + hb_ref[...]) * gate_ref[...].astype(F32)).astype(BF16)
    y = (jnp.dot(rnn, wo_ref[0:D_RNN, :], preferred_element_type=F32)
         + jnp.dot(att_ref[...], wo_ref[D_RNN:, :], preferred_element_type=F32))
    _finish_mixer(y, x_ref, mod_ref, g_ref, rwh_ref, rwl_ref, rb_ref, xn_ref, h2_ref, route_ref)


def _na_out_kernel(att_ref, x_ref, mod_ref, g_ref, wo_ref, rwh_ref, rwl_ref, rb_ref, xn_ref, h2_ref, route_ref):
    x2d = x_ref.at[0] if len(x_ref.shape) == 3 else x_ref
    for sub in range(att_ref.shape[0] // TM):
        r = pl.ds(sub * TM, TM)
        y = jnp.dot(att_ref[r, :], wo_ref[...], preferred_element_type=F32)
        _finish_mixer(y, x2d.at[r, :], mod_ref, g_ref, rwh_ref, rwl_ref, rb_ref,
                      xn_ref.at[r, :], h2_ref.at[r, :], route_ref.at[:, r])


def _mixer_out(kernel, acts, x_all, mod, g, wo, rw_hi, rw_lo, rb, off):
    nb, t, _ = x_all.shape
    nt = t // TM - off
    sub = 2 if (off == 1 and nt % 2 == 0 and kernel is _na_out_kernel) else 1
    rows = sub * TM
    out_row = lambda b, i: (b, i, 0)
    const = lambda b, i: (0, 0)
    if sub == 1:
        row = lambda b, i: (b, i + off, 0)
        x_spec = pl.BlockSpec((None, TM, D), row)
    else:
        row = None
        x_spec = pl.BlockSpec((pl.Element(1), pl.Element(rows), pl.Element(D)),
                              lambda b, i: (b, (sub * i + off) * TM, 0))
    act_specs = [pl.BlockSpec((None, rows, a.shape[2]), row if a.shape[1] == t else out_row) for a in acts]
    assert sub == 1 or all(a.shape[1] != t for a in acts)
    return pl.pallas_call(
        kernel,
        grid=(nb, nt // sub),
        in_specs=act_specs + [
            x_spec,
            _mod_spec(nb, off),
            _resident((1, D), const),
            _resident(wo.shape, const),
            _resident((N_EXP, D), const),
            _resident((2 * N_EXP, D), const),
            _resident((N_EXP, 1), const),
        ],
        out_specs=[
            pl.BlockSpec((None, rows, D), out_row),
            pl.BlockSpec((None, rows, D), out_row),
            pl.BlockSpec((None, 8, rows), lambda b, i: (b, 0, i)),
        ],
        out_shape=[
            jax.ShapeDtypeStruct((nb, nt * TM, D), F32),
            jax.ShapeDtypeStruct((nb, nt * TM, D), F32),
            jax.ShapeDtypeStruct((nb, 8, nt * TM), F32),
        ],
        compiler_params=_cparams(("arbitrary", "arbitrary")),
        name="mixer_out",
    )(*acts, x_all, mod, g, wo, rw_hi, rw_lo, rb)


def _na_in_kernel(x_ref, g_ref, mod_ref, w_ref, hg_ref, o_ref):
    h = _modulate(x_ref[...], g_ref[...], mod_ref[:, 0:D], mod_ref[:, D:2 * D]).astype(BF16)
    for p in range(3):
        u = jnp.dot(h, w_ref[:, p * D:(p + 1) * D], preferred_element_type=F32)
        if p == 2:
            o_ref[p] = u.astype(BF16)
            continue
        hg = hg_ref[p]
        for hh in range(NA_H):
            uh = u[:, hh * NA_DH:(hh + 1) * NA_DH]
            o_ref[p, :, hh * NA_DH:(hh + 1) * NA_DH] = _rms(uh, hg, NA_DH).astype(BF16)


def _na_in(x_all, g, mod, w_qkv, hg):
    nb, t, _ = x_all.shape
    nt = t // TM
    return pl.pallas_call(
        _na_in_kernel,
        grid=(nb, nt),
        in_specs=[
            pl.BlockSpec((None, TM, D), lambda b, i: (b, i, 0)),
            _resident((1, D), lambda b, i: (0, 0)),
            _mod_spec(nb, 0),
            _resident((D, 3 * D), lambda b, i: (0, 0)),
            _resident((3, 1, NA_DH), lambda b, i: (0, 0, 0)),
        ],
        out_specs=pl.BlockSpec((3, None, TM, D), lambda b, i: (0, b, i, 0)),
        out_shape=jax.ShapeDtypeStruct((3, nb, t, D), BF16),
        compiler_params=_cparams(("arbitrary", "arbitrary")),
        name="na_in",
    )(x_all, g, mod, w_qkv, hg)


def _na_key_base(rb, rows):
    return jnp.clip(rb * NA_QR - NA_ROWS // 2, 0, rows - NA_KR)


NA_HPS = 4


def _na_attn_kernel(var_ref, q_ref, k_ref, v_ref, bias_ref, o_ref, *, rows):
    rb = pl.program_id(2)
    start = pl.multiple_of(TM + _na_key_base(rb, rows) * GRID_W, GRID_W)
    nk = NA_KR * GRID_W
    nt_dims = (((1,), (1,)), ((), ()))
    for hh in range(NA_HPS):
        hs = slice(hh * NA_DH, (hh + 1) * NA_DH)
        q = q_ref[:, hs]
        s_c = lax.dot_general(q, k_ref[0:TM, hs], nt_dims, preferred_element_type=F32)
        s_l = lax.dot_general(q, k_ref[pl.ds(start, nk), hs], nt_dims, preferred_element_type=F32) + bias_ref[hh]
        m = jnp.maximum(jnp.max(s_c, axis=-1, keepdims=True), jnp.max(s_l, axis=-1, keepdims=True))
        p_c = jnp.exp2(s_c - m)
        p_l = jnp.exp2(s_l - m)
        l = jnp.sum(p_c, axis=-1, keepdims=True) + jnp.sum(p_l, axis=-1, keepdims=True)
        o = (jnp.dot(p_l.astype(BF16), v_ref[pl.ds(start, nk), hs], preferred_element_type=F32)
             + jnp.dot(p_c.astype(BF16), v_ref[0:TM, hs], preferred_element_type=F32))
        o_ref[:, hs] = (o / l).astype(BF16)


def _na_attn(qkv, bias, variant):
    _, nb, t, _ = qkv.shape
    rows = (t - TM) // GRID_W
    nrb = rows // NA_QR
    nq = NA_QR * GRID_W
    assert nq == TM
    return pl.pallas_call(
        functools.partial(_na_attn_kernel, rows=rows),
        grid_spec=pltpu.PrefetchScalarGridSpec(
            num_scalar_prefetch=1,
            grid=(NA_H // NA_HPS, nb, nrb),
            in_specs=[
                pl.BlockSpec((None, None, nq, NA_HPS * NA_DH), lambda h, b, r, var: (0, b, r + 1, h)),
                pl.BlockSpec((None, None, t, NA_HPS * NA_DH), lambda h, b, r, var: (1, b, 0, h)),
                pl.BlockSpec((None, None, t, NA_HPS * NA_DH), lambda h, b, r, var: (2, b, 0, h)),
                pl.BlockSpec((NA_HPS, None, nq, NA_KR * GRID_W), lambda h, b, r, var: (h, var[r], 0, 0)),
            ],
            out_specs=pl.BlockSpec((None, nq, NA_HPS * NA_DH), lambda h, b, r, var: (b, r, h)),
        ),
        out_shape=jax.ShapeDtypeStruct((nb, t - TM, D), BF16),
        compiler_params=_cparams(("arbitrary", "arbitrary", "arbitrary")),
        name="na_attn",
    )(variant, qkv, qkv, qkv, bias)


def _na_bias_tables(rpb, rows):
    nrb = rows // NA_QR
    rb = np.arange(nrb)
    kb = np.clip(rb * NA_QR - NA_ROWS // 2, 0, rows - NA_KR)
    r_q = rb[:, None] * NA_QR + np.arange(NA_QR)[None, :]
    r0_q = np.clip(r_q - NA_ROWS // 2, 0, rows - NA_ROWS)
    sig = np.concatenate([r_q - kb[:, None], r0_q - kb[:, None]], axis=1)
    uniq, variant = np.unique(sig, axis=0, return_inverse=True)
    variant = np.asarray(variant).reshape(-1)
    kr = np.arange(NA_KR)[None, :]
    cols = np.arange(GRID_W)
    col_start = np.clip(cols - NA_COLS // 2, 0, GRID_W - NA_COLS)
    kc = cols[None, :]
    col_ok = (kc >= col_start[:, None]) & (kc < col_start[:, None] + NA_COLS)
    nh, ndr, ndc = rpb.shape
    per = 2 * GRID_W + 1
    p = jnp.concatenate([rpb[:, :, NA_COLS - 1:], jnp.zeros((nh, ndr, per - ndc), F32), rpb[:, :, :NA_COLS - 1]], axis=2)
    toep = jnp.tile(p, (1, 1, GRID_W))[:, :, :GRID_W * (per - 1)].reshape(nh, ndr, GRID_W, per - 1)[:, :, :, :GRID_W]
    toep = jnp.where(col_ok[None, None], toep, NEG)
    masked = jnp.full((nh, GRID_W, GRID_W), NEG, F32)
    tabs = []
    for u in uniq:
        r_abs = u[:NA_QR][:, None]
        r0 = u[NA_QR:][:, None]
        row_ok = (kr >= r0) & (kr < r0 + NA_ROWS)
        drow = kr - r_abs + NA_ROWS - 1
        blocks = [jnp.stack([toep[:, int(drow[a, b])] if row_ok[a, b] else masked for b in range(NA_KR)], axis=2)
                  for a in range(NA_QR)]
        tabs.append(jnp.stack(blocks, axis=1).reshape(nh, NA_QR * GRID_W, NA_KR * GRID_W))
    return jnp.stack(tabs, axis=1), jnp.asarray(variant, I32)


def _swiglu_tile(x_bf16, wg, wu, wd):
    gate = jnp.dot(x_bf16, wg, preferred_element_type=F32)
    up = jnp.dot(x_bf16, wu, preferred_element_type=F32)
    hmid = (_silu(gate) * up).astype(BF16)
    return jnp.dot(hmid, wd, preferred_element_type=F32)


def _invert_kernel(pos_ref, tok_ref, *, n_rows):
    def clear(r, c):
        tok_ref[r] = 0
        return c

    def place(j, c):
        tok_ref[pos_ref[j]] = lax.shift_right_logical(j, 1)
        return c

    lax.fori_loop(0, n_rows, clear, 0, unroll=16)
    lax.fori_loop(0, pos_ref.shape[0], place, 0, unroll=16)


def _invert(pos, n_rows):
    return pl.pallas_call(
        functools.partial(_invert_kernel, n_rows=n_rows),
        grid_spec=pltpu.PrefetchScalarGridSpec(
            num_scalar_prefetch=1,
            grid=(1,),
            in_specs=[],
            out_specs=pl.BlockSpec(memory_space=pltpu.SMEM),
        ),
        out_shape=jax.ShapeDtypeStruct((n_rows,), I32),
        compiler_params=_cparams(("arbitrary",)),
        name="moe_invert",
    )(pos)


def _expert_kernel(te_ref, nu_ref, tok_ref, h2_ref, wg_ref, wu_ref, wd_ref, y_ref, xbuf, sem):
    g = pl.program_id(0)
    nu = nu_ref[0]
    slot = lax.rem(g, 2)

    def row_copy(tile, buf, j):
        return pltpu.make_async_copy(h2_ref.at[pl.ds(tok_ref[tile * TM + j], 1), :],
                                     xbuf.at[buf, pl.ds(j, 1), :], sem.at[buf])

    def gather(tile, buf):
        for j in range(TM):
            row_copy(tile, buf, j).start()

    def wait(tile, buf):
        for j in range(TM):
            row_copy(tile, buf, j).wait()

    @pl.when(g == 0)
    def _():
        gather(0, 0)

    @pl.when(g < nu)
    def _():
        nxt = jnp.minimum(g + 1, nu - 1)
        gather(nxt, 1 - slot)
        wait(g, slot)
        y_ref[...] = _swiglu_tile(xbuf[slot].astype(BF16), wg_ref[...], wu_ref[...], wd_ref[...])

    @pl.when(g == nu - 1)
    def _():
        wait(g, 1 - slot)

    @pl.when(g >= nu)
    def _():
        y_ref[...] = jnp.zeros_like(y_ref)


def _experts(h2, tok, tile_expert, n_used, wg, wu, wd, layer):
    nb, t, _ = h2.shape
    n_rows = tok.shape[0]
    n_tiles = n_rows // TM
    return pl.pallas_call(
        _expert_kernel,
        grid_spec=pltpu.PrefetchScalarGridSpec(
            num_scalar_prefetch=3,
            grid=(n_tiles,),
            in_specs=[
                pl.BlockSpec(memory_space=pl.ANY),
                pl.BlockSpec((None, None, D, D_EXP), lambda g, te, nu, tk: (layer, te[g], 0, 0)),
                pl.BlockSpec((None, None, D, D_EXP), lambda g, te, nu, tk: (layer, te[g], 0, 0)),
                pl.BlockSpec((None, None, D_EXP, D), lambda g, te, nu, tk: (layer, te[g], 0, 0)),
            ],
            out_specs=pl.BlockSpec((TM, D), lambda g, te, nu, tk: (g, 0)),
            scratch_shapes=[pltpu.VMEM((2, TM, D), F32), pltpu.SemaphoreType.DMA((2,))],
        ),
        out_shape=jax.ShapeDtypeStruct((n_rows, D), F32),
        compiler_params=_cparams(("arbitrary",)),
        name="moe_experts",
    )(tile_expert, n_used, tok, h2.reshape(nb * t, D), wg, wu, wd)


def _combine_kernel(pos_ref, h2_ref, xn_ref, mod_ref, wt_ref, wsg_ref, wsu_ref, wsd_ref, ys_ref, o_ref, ybuf, sem, *, nt):
    tile = pl.program_id(0) * nt + pl.program_id(1)
    base = tile * (2 * TM)

    def row_copy(j, kk):
        return pltpu.make_async_copy(ys_ref.at[pl.ds(pos_ref[base + 2 * j + kk], 1), :], ybuf.at[kk, pl.ds(j, 1), :], sem)

    for j in range(TM):
        row_copy(j, 0).start()
        row_copy(j, 1).start()
    shared = _swiglu_tile(h2_ref[...].astype(BF16), wsg_ref[...], wsu_ref[...], wsd_ref[...])
    for j in range(TM):
        row_copy(j, 0).wait()
        row_copy(j, 1).wait()
    wt = wt_ref[...]
    y = shared + wt[:, 0:1] * ybuf[0] + wt[:, 1:2] * ybuf[1]
    o_ref[...] = xn_ref[...] + mod_ref[:, 5 * D:6 * D] * y


def _combine(pos, h2, xn, mod, wt, wsg, wsu, wsd, ys, off, layer):
    nb, t, _ = h2.shape
    nt = t // TM
    row = lambda b, i, *_: (b, i, 0)
    const = lambda b, i, *_: (layer, 0, 0)
    return pl.pallas_call(
        functools.partial(_combine_kernel, nt=nt),
        grid_spec=pltpu.PrefetchScalarGridSpec(
            num_scalar_prefetch=1,
            grid=(nb, nt),
            in_specs=[
                pl.BlockSpec((None, TM, D), row),
                pl.BlockSpec((None, TM, D), row),
                pl.BlockSpec((None, 1, ADA * D), lambda b, i, *_: (jnp.where(i + off == 0, nb, b), 0, 0)),
                pl.BlockSpec((None, TM, 8), row),
                _resident((None, D, D_EXP), const),
                _resident((None, D, D_EXP), const),
                _resident((None, D_EXP, D), const),
                pl.BlockSpec(memory_space=pl.ANY),
            ],
            out_specs=pl.BlockSpec((None, TM, D), row),
            scratch_shapes=[pltpu.VMEM((2, TM, D), F32), pltpu.SemaphoreType.DMA],
        ),
        out_shape=jax.ShapeDtypeStruct((nb, t, D), F32),
        compiler_params=_cparams(("arbitrary", "arbitrary")),
        name="moe_combine",
    )(pos, h2, xn, mod, wt, wsg, wsu, wsd, ys)


def _moe_plan(route):
    nb, _, tp = route.shape
    n_tok = nb * tp
    idx = route[:, 0:2, :].astype(I32).transpose(0, 2, 1).reshape(n_tok * 2)
    wt = jnp.pad(route[:, 2:4, :].transpose(0, 2, 1), ((0, 0), (0, 0), (0, 6)))
    onehot = (idx[:, None] == jnp.arange(N_EXP, dtype=I32)[None, :]).astype(I32)
    csum = jnp.cumsum(onehot, axis=0)
    counts = csum[-1]
    rank = jnp.sum((csum - onehot) * onehot, axis=1)
    padded = ((counts + TM - 1) // TM) * TM
    ends = jnp.cumsum(padded)
    starts = ends - padded
    pos = (jnp.sum(onehot * starts[None, :], axis=1) + rank).astype(I32)
    n_tiles = (2 * n_tok) // TM + N_EXP
    n_used = (ends[-1] // TM).astype(I32)
    tile_block = jnp.minimum(jnp.arange(n_tiles, dtype=I32), n_used - 1)
    tile_expert = jnp.sum((tile_block[:, None] * TM >= ends[None, :]).astype(I32), axis=1).astype(I32)
    return pos, tile_expert, n_used.reshape(1), wt, n_tiles * TM


def _moe(h2, xn, route, mod, w, off, layer):
    pos, tile_expert, n_used, wt, n_rows = _moe_plan(route)
    tok = _invert(pos, n_rows)
    ys = _experts(h2, tok, tile_expert, n_used, w["wg"], w["wu"], w["wd"], layer)
    return _combine(pos, h2, xn, mod, wt, w["wsg"], w["wsu"], w["wsd"], ys, off, layer)


def _rope_tables(t, s):
    half = MLA_ROPE // 2
    inv_freq = ROPE_THETA ** (-jnp.arange(0, half, 2, dtype=F32) / half)
    tt = jnp.arange(t - s)
    ang_row = (tt // GRID_W).astype(F32)[:, None] * inv_freq
    ang_col = (tt % GRID_W).astype(F32)[:, None] * inv_freq
    ang = jnp.concatenate([ang_row] * 2 + [ang_col] * 2, axis=1)
    cos = jnp.cos(ang)
    sin = jnp.sin(ang)
    sign = np.where((np.arange(MLA_ROPE) % 32) < 16, -1.0, 1.0).astype(np.float32)
    sin = sin * sign[None, :]
    cos = jnp.concatenate([jnp.ones((s, MLA_ROPE), F32), cos], axis=0)
    sin = jnp.concatenate([jnp.zeros((s, MLA_ROPE), F32), sin], axis=0)
    cos = jnp.pad(cos, ((0, 0), (0, LANE - MLA_ROPE)), constant_values=1.0)
    sin = jnp.pad(sin, ((0, 0), (0, LANE - MLA_ROPE)))
    return cos.astype(F32), sin.astype(F32)


def _ab_weights(j, t, s, ab_w_in, lru_conv_w, lru_conv_b, lru_w_a, lru_b_a, lru_w_i, lru_b_i, lru_lambda,
                mla_q_norm_g, mla_w_uq, mla_kv_norm_g, mla_w_ukv, mla_q_head_g, mla_k_head_g):
    w_in = jnp.pad(ab_w_in[j], ((0, 0), (0, AB_IN_PAD - AB_IN))).astype(BF16)
    uq = mla_w_uq[j].reshape(Q_RANK, MLA_H, MLA_QK)
    uq_n = uq[:, :, :MLA_NOPE].reshape(Q_RANK, MLA_H * LANE)
    uq_r = jnp.pad(uq[:, :, MLA_NOPE:], ((0, 0), (0, 0), (0, LANE - MLA_ROPE))).reshape(Q_RANK, MLA_H * LANE)
    ukv = mla_w_ukv[j].reshape(KV_RANK, MLA_H, MLA_NOPE + MLA_V)
    ukv_k = ukv[:, :, :MLA_NOPE].reshape(KV_RANK, MLA_H * LANE)
    ukv_v = ukv[:, :, MLA_NOPE:].reshape(KV_RANK, MLA_H * MLA_V)
    scale = MLA_QK ** -0.5 * LOG2E
    qg = mla_q_head_g[j] * scale
    kg = mla_k_head_g[j]
    pad_r = lambda g: jnp.pad(g[MLA_NOPE:], (0, LANE - MLA_ROPE))
    hg = jnp.stack([qg[:MLA_NOPE], pad_r(qg), kg[:MLA_NOPE], pad_r(kg)], axis=0)
    cos, sin = _rope_tables(t, s)
    return dict(
        w_in=w_in, qng=mla_q_norm_g[j][None], kvng=mla_kv_norm_g[j][None],
        w_uq=jnp.concatenate([uq_n, uq_r], axis=1).astype(BF16),
        w_ukv=jnp.concatenate([ukv_k, ukv_v], axis=1).astype(BF16),
        hg=hg, cos=cos, sin=sin,
        conv_w=lru_conv_w[j], conv_b=lru_conv_b[j][None],
        w_a=lru_w_a[j].astype(BF16), b_a=lru_b_a[j][:, None, :],
        w_i=lru_w_i[j].astype(BF16), b_i=lru_b_i[j][:, None, :], lam=lru_lambda[j][:, None, :],
    )


def kernel(x, c, ctx, c_ctx, ada_w, ada_b, norm_mix_g, norm_ffn_g, ab_w_in, ab_w_out, lru_conv_w, lru_conv_b, lru_w_a, lru_b_a, lru_w_i, lru_b_i, lru_lambda, mla_q_norm_g, mla_w_uq, mla_kv_norm_g, mla_w_ukv, mla_q_head_g, mla_k_head_g, na_w_qkv, na_w_out, na_q_head_g, na_k_head_g, na_rpb, router_w, router_b, moe_w_gate, moe_w_up, moe_w_down, moe_ws_gate, moe_ws_up, moe_ws_down):
    nb, seq, _ = x.shape
    s = ctx.shape[1]
    depth = ada_w.shape[0]
    assert s == TM and seq % TM == 0 and nb < 8
    t = s + seq
    rows = seq // GRID_W

    cs = jnp.zeros((8, D), F32).at[:nb].set(c).at[nb].set(c_ctx)
    mod_all = _adaln(cs, ada_w, ada_b).reshape(depth, 8, 1, ADA * D)

    rw_t = router_w.T
    rw_hi = rw_t.astype(BF16)
    rw_lo = jnp.concatenate([rw_hi, (rw_t - rw_hi.astype(F32)).astype(BF16)], axis=0)
    rb = router_b.reshape(N_EXP, 1)

    moe_w = dict(wg=moe_w_gate.astype(BF16), wu=moe_w_up.astype(BF16), wd=moe_w_down.astype(BF16),
                 wsg=moe_ws_gate.astype(BF16), wsu=moe_ws_up.astype(BF16), wsd=moe_ws_down.astype(BF16))

    x_all = jnp.concatenate([ctx, x], axis=1)
    for i in range(depth):
        last = i == depth - 1
        off = 1 if last else 0
        j = i // 2
        mod = mod_all[i]
        g_mix = norm_mix_g[i][None]
        g_ffn = norm_ffn_g[i][None]
        if i % 2 == 0:
            w = _ab_weights(j, t, s, ab_w_in, lru_conv_w, lru_conv_b, lru_w_a, lru_b_a, lru_w_i, lru_b_i, lru_lambda,
                            mla_q_norm_g, mla_w_uq, mla_kv_norm_g, mla_w_ukv, mla_q_head_g, mla_k_head_g)
            ux, gate, q, k, v = _ab_in(x_all, g_mix, mod, w)
            h_f, h_b = _lru(ux, w)
            att = _mla_attn(q, k, v)
            xn, h2, route = _mixer_out(_ab_out_kernel, (h_f, h_b, gate, att),
                                       x_all, mod, g_ffn, ab_w_out[j].astype(BF16), rw_hi, rw_lo, rb, off)
        else:
            scale = NA_DH ** -0.5 * LOG2E
            hg = jnp.stack([na_q_head_g[j] * scale, na_k_head_g[j], jnp.ones((NA_DH,), F32)], axis=0)[:, None, :]
            qkv = _na_in(x_all, g_mix, mod, na_w_qkv[j].astype(BF16), hg)
            bias, variant = _na_bias_tables(na_rpb[j] * LOG2E, rows)
            att = _na_attn(qkv, bias, variant)
            assert last, "the neighbourhood layer has no context-output path"
            xn, h2, route = _mixer_out(_na_out_kernel, (att,), x_all, mod, g_ffn, na_w_out[j].astype(BF16),
                                       rw_hi, rw_lo, rb, off)
        x_all = _moe(h2, xn, route, mod, moe_w, off, i)
    return x_all
```

```python
import functools

import numpy as np
import jax
import jax.numpy as jnp
from jax import lax
from jax.experimental import pallas as pl
from jax.experimental.pallas import tpu as pltpu

F32 = jnp.float32
BF16 = jnp.bfloat16
I32 = jnp.int32

D = 2048
ADA = 6
EPS = 1e-6
NEG = -1e30
LOG2E = 1.4426950408889634
GRID_W = 64
D_RNN = 1024
LRU_BLOCKS = 8
LRU_BW = 128
LRU_C = 8.0
MLA_H = 8
MLA_NOPE = 128
MLA_ROPE = 64
MLA_QK = 192
MLA_V = 128
Q_RANK = 512
KV_RANK = 512
ROPE_THETA = 10000.0
AB_IN = 3136
AB_IN_PAD = 3200
NA_H = 16
NA_DH = 128
NA_ROWS = 8
NA_COLS = 16
N_EXP = 16
N_GROUPS = 4
EXP_PER_GROUP = 4
D_EXP = 1408

LANE = 128
SUBLANES = 8
TM = 256
TE = 256
VMEM_LIMIT = 56 * 1024 * 1024

NA_QR = 4
NA_KR = 12


def _cparams(sem, vmem=VMEM_LIMIT):
    return pltpu.CompilerParams(dimension_semantics=sem, vmem_limit_bytes=vmem)


def _resident(shape, index_map):
    return pl.BlockSpec(shape, index_map, pipeline_mode=pl.Buffered(1))


def _rms(x, g, n):
    ms = jnp.sum(x * x, axis=-1, keepdims=True) * (1.0 / n)
    return (x * lax.rsqrt(ms + EPS)) * g


def _modulate(x, g, shift, scale):
    return _rms(x, g, D) * (1.0 + scale) + shift


def _gelu_tanh(x):
    c = 0.7978845608028654
    return x * (0.5 * (1.0 + jnp.tanh(c * (x + 0.044715 * (x * x * x)))))


def _silu(x):
    return x * jax.nn.sigmoid(x)


ADA_TN = 1536


def _adaln_kernel(cs_ref, w_ref, b_ref, o_ref):
    a = _silu(cs_ref[...]).astype(BF16)
    o_ref[...] = jnp.dot(a, w_ref[...].astype(BF16), preferred_element_type=F32) + b_ref[...]


def _adaln(cs, ada_w, ada_b):
    depth = ada_w.shape[0]
    n = ADA * D
    return pl.pallas_call(
        _adaln_kernel,
        grid=(depth, n // ADA_TN),
        in_specs=[
            pl.BlockSpec((8, D), lambda l, j: (0, 0)),
            pl.BlockSpec((None, D, ADA_TN), lambda l, j: (l, 0, j)),
            pl.BlockSpec((None, 1, ADA_TN), lambda l, j: (l, 0, j)),
        ],
        out_specs=pl.BlockSpec((None, 8, ADA_TN), lambda l, j: (l, 0, j)),
        out_shape=jax.ShapeDtypeStruct((depth, 8, n), F32),
        compiler_params=_cparams(("arbitrary", "arbitrary")),
        name="adaln",
    )(cs, ada_w, ada_b.reshape(depth, 1, n))


def _mod_spec(nb, off):
    return pl.BlockSpec((None, 1, ADA * D), lambda b, i: (jnp.where(i + off == 0, nb, b), 0, 0))


def _split_x(x_in, off):
    ctx_map = lambda b, i: (b, 0, 0)
    if isinstance(x_in, tuple):
        return x_in, ctx_map, lambda b, i: (b, jnp.maximum(i + off - 1, 0), 0)
    return (x_in, x_in), ctx_map, lambda b, i: (b, jnp.maximum(i + off, 1), 0)


def _ab_in_kernel(xc_ref, xl_ref, g_ref, mod_ref, win_ref, qng_ref, kvng_ref, wuq_ref, wukv_ref, hg_ref, cos_ref,
                  sin_ref, ux_ref, gate_ref, q_ref, k_ref, v_ref):
    x = jnp.where(pl.program_id(1) == 0, xc_ref[...], xl_ref[...])
    h = _modulate(x, g_ref[...], mod_ref[:, 0:D], mod_ref[:, D:2 * D]).astype(BF16)
    u = jnp.dot(h, win_ref[...], preferred_element_type=F32)
    ux_ref[...] = u[:, 0:D_RNN].astype(BF16)
    gate_ref[...] = _gelu_tanh(u[:, D_RNN:2 * D_RNN]).astype(BF16)
    o2 = 2 * D_RNN
    cq = _rms(u[:, o2:o2 + Q_RANK], qng_ref[...], Q_RANK).astype(BF16)
    ckv = _rms(u[:, o2 + Q_RANK:o2 + Q_RANK + KV_RANK], kvng_ref[...], KV_RANK).astype(BF16)
    qf = jnp.dot(cq, wuq_ref[...], preferred_element_type=F32)
    kvf = jnp.dot(ckv, wukv_ref[...], preferred_element_type=F32)
    kr = u[:, o2 + Q_RANK + KV_RANK:AB_IN_PAD]
    cos = cos_ref[...]
    sin = sin_ref[...]
    lane = lax.broadcasted_iota(I32, (TM, LANE), 1)
    first_half = (lane % 32) < 16

    def rope(z):
        partner = jnp.where(first_half, pltpu.roll(z, LANE - 16, 1), pltpu.roll(z, 16, 1))
        return z * cos + partner * sin

    hg = hg_ref[...]
    kr_ss = jnp.sum(kr * kr, axis=-1, keepdims=True)
    nh = MLA_H * MLA_NOPE
    for hh in range(MLA_H):
        qn = qf[:, hh * LANE:(hh + 1) * LANE]
        qr = qf[:, nh + hh * LANE:nh + (hh + 1) * LANE]
        ss = jnp.sum(qn * qn, axis=-1, keepdims=True) + jnp.sum(qr * qr, axis=-1, keepdims=True)
        inv = lax.rsqrt(ss * (1.0 / MLA_QK) + EPS)
        q_ref[:, hh * 256:hh * 256 + LANE] = ((qn * inv) * hg[0:1]).astype(BF16)
        q_ref[:, hh * 256 + LANE:(hh + 1) * 256] = rope((qr * inv) * hg[1:2]).astype(BF16)
        kn = kvf[:, hh * LANE:(hh + 1) * LANE]
        ss = jnp.sum(kn * kn, axis=-1, keepdims=True) + kr_ss
        inv = lax.rsqrt(ss * (1.0 / MLA_QK) + EPS)
        k_ref[:, hh * 256:hh * 256 + LANE] = ((kn * inv) * hg[2:3]).astype(BF16)
        k_ref[:, hh * 256 + LANE:(hh + 1) * 256] = rope((kr * inv) * hg[3:4]).astype(BF16)
    v_ref[...] = kvf[:, nh:].astype(BF16)


def _ab_in(x_in, nb, t, g, mod, w):
    nt = t // TM
    (x_c, x_l), ctx_map, lat_map = _split_x(x_in, 0)
    row = lambda b, i: (b, i, 0)
    const = lambda b, i: (0, 0)
    outs = [
        jax.ShapeDtypeStruct((nb, t, D_RNN), BF16),
        jax.ShapeDtypeStruct((nb, t, D_RNN), BF16),
        jax.ShapeDtypeStruct((nb, t, MLA_H * 256), BF16),
        jax.ShapeDtypeStruct((nb, t, MLA_H * 256), BF16),
        jax.ShapeDtypeStruct((nb, t, MLA_H * MLA_V), BF16),
    ]
    return pl.pallas_call(
        _ab_in_kernel,
        grid=(nb, nt),
        in_specs=[
            pl.BlockSpec((None, TM, D), ctx_map),
            pl.BlockSpec((None, TM, D), lat_map),
            _resident((1, D), const),
            _mod_spec(nb, 0),
            _resident((D, AB_IN_PAD), const),
            _resident((1, Q_RANK), const),
            _resident((1, KV_RANK), const),
            _resident((Q_RANK, 2 * MLA_H * LANE), const),
            _resident((KV_RANK, 2 * MLA_H * LANE), const),
            _resident((4, LANE), const),
            pl.BlockSpec((TM, LANE), lambda b, i: (i, 0)),
            pl.BlockSpec((TM, LANE), lambda b, i: (i, 0)),
        ],
        out_specs=[
            pl.BlockSpec((None, TM, D_RNN), row),
            pl.BlockSpec((None, TM, D_RNN), row),
            pl.BlockSpec((None, TM, MLA_H * 256), row),
            pl.BlockSpec((None, TM, MLA_H * 256), row),
            pl.BlockSpec((None, TM, MLA_H * MLA_V), row),
        ],
        out_shape=outs,
        compiler_params=_cparams(("arbitrary", "arbitrary")),
        name="ab_in",
    )(x_c, x_l, g, mod, w["w_in"], w["qng"], w["kvng"], w["w_uq"], w["w_ukv"], w["hg"], w["cos"], w["sin"])


def _lru_dir(prev_ref, cur_ref, next_ref, c, n_chunks, d, reverse,
             cw_ref, cb_ref, wa_ref, ba_ref, wi_ref, bi_ref, lam_ref, carry_ref, out_ref):
    xc = cur_ref[...].astype(F32)
    row = lax.broadcasted_iota(I32, (TM, 1), 0)
    left_ok = jnp.where(c >= 2, 1.0, 0.0)
    right_ok = jnp.where(jnp.logical_and(c >= 1, c <= n_chunks - 2), 1.0, 0.0)
    p0 = prev_ref[TM - 2:TM - 1, :].astype(F32) * left_ok
    p1 = prev_ref[TM - 1:TM, :].astype(F32) * left_ok
    n0 = next_ref[0:1, :].astype(F32) * right_ok
    x_m2 = jnp.where(row >= 2, pltpu.roll(xc, 2, 0), jnp.where(row == 0, p0, p1))
    x_m1 = jnp.where(row >= 1, pltpu.roll(xc, 1, 0), p1)
    x_p1 = jnp.where(row <= TM - 2, pltpu.roll(xc, TM - 1, 0), n0)
    cw = cw_ref[...]
    xconv = x_m2 * cw[0:1] + x_m1 * cw[1:2] + xc * cw[2:3] + x_p1 * cw[3:4] + cb_ref[...]

    xb = xconv.astype(BF16)
    ra = []
    ri = []
    for n in range(LRU_BLOCKS):
        xs = xb[:, n * LRU_BW:(n + 1) * LRU_BW]
        ra.append(jnp.dot(xs, wa_ref[d, n], preferred_element_type=F32))
        ri.append(jnp.dot(xs, wi_ref[d, n], preferred_element_type=F32))
    r = jax.nn.sigmoid(jnp.concatenate(ra, axis=1) + ba_ref[d])
    gi = jax.nn.sigmoid(jnp.concatenate(ri, axis=1) + bi_ref[d])
    neg_lam = -lam_ref[d]
    softplus = jnp.maximum(neg_lam, 0.0) + jnp.log1p(jnp.exp(-jnp.abs(neg_lam)))
    log_a = (-LRU_C * r) * softplus
    a = jnp.exp(log_a)
    th = jnp.tanh(log_a)
    neg_expm1 = (-2.0 * th) / (1.0 - th)
    bb = jnp.sqrt(neg_expm1) * gi * xconv

    ng = TM // SUBLANES
    a3 = a.reshape(ng, SUBLANES, D_RNN)
    b3 = bb.reshape(ng, SUBLANES, D_RNN)
    sub = lax.broadcasted_iota(I32, (1, SUBLANES, 1), 1)
    k = 1
    while k < SUBLANES:
        if reverse:
            keep = sub < SUBLANES - k
            a_s = jnp.where(keep, pltpu.roll(a3, SUBLANES - k, 1), 1.0)
            b_s = jnp.where(keep, pltpu.roll(b3, SUBLANES - k, 1), 0.0)
        else:
            keep = sub >= k
            a_s = jnp.where(keep, pltpu.roll(a3, k, 1), 1.0)
            b_s = jnp.where(keep, pltpu.roll(b3, k, 1), 0.0)
        b3 = b3 + a3 * b_s
        a3 = a3 * a_s
        k *= 2
    hcar = carry_ref[d:d + 1, :]
    last = 0 if reverse else SUBLANES - 1
    for g in (range(ng - 1, -1, -1) if reverse else range(ng)):
        h = b3[g] + a3[g] * hcar
        out_ref[g * SUBLANES:(g + 1) * SUBLANES, :] = h
        hcar = h[last:last + 1, :]
    carry_ref[d:d + 1, :] = hcar


def _lru_kernel(pf_ref, cf_ref, nf_ref, pb_ref, cbk_ref, nb_ref, cw_ref, cb_ref, wa_ref, ba_ref, wi_ref, bi_ref, lam_ref,
                hf_ref, hb_ref, carry_ref, *, n_chunks):
    s = pl.program_id(1)

    @pl.when(s == 0)
    def _():
        carry_ref[...] = jnp.zeros_like(carry_ref)

    params = (cw_ref, cb_ref, wa_ref, ba_ref, wi_ref, bi_ref, lam_ref, carry_ref)
    _lru_dir(pf_ref, cf_ref, nf_ref, s, n_chunks, 0, False, *params, hf_ref)
    cb_idx = jnp.where(s == 0, 0, n_chunks - s)
    _lru_dir(pb_ref, cbk_ref, nb_ref, cb_idx, n_chunks, 1, True, *params, hb_ref)


def _lru(ux, w):
    nb, t, _ = ux.shape
    nt = t // TM
    fwd = lambda s: s
    bwd = lambda s: jnp.where(s == 0, 0, nt - s)
    prev = lambda c: jnp.maximum(c - 1, 0)
    nxt = lambda c: jnp.minimum(c + 1, nt - 1)
    blk = (None, TM, D_RNN)
    const2 = lambda b, s: (0, 0)
    const3 = lambda b, s: (0, 0, 0)
    const4 = lambda b, s: (0, 0, 0, 0)
    return pl.pallas_call(
        functools.partial(_lru_kernel, n_chunks=nt),
        grid=(nb, nt),
        in_specs=[
            pl.BlockSpec(blk, lambda b, s: (b, prev(fwd(s)), 0)),
            pl.BlockSpec(blk, lambda b, s: (b, fwd(s), 0)),
            pl.BlockSpec(blk, lambda b, s: (b, nxt(fwd(s)), 0)),
            pl.BlockSpec(blk, lambda b, s: (b, prev(bwd(s)), 0)),
            pl.BlockSpec(blk, lambda b, s: (b, bwd(s), 0)),
            pl.BlockSpec(blk, lambda b, s: (b, nxt(bwd(s)), 0)),
            _resident((4, D_RNN), const2),
            _resident((1, D_RNN), const2),
            _resident((2, LRU_BLOCKS, LRU_BW, LRU_BW), const4),
            _resident((2, 1, D_RNN), const3),
            _resident((2, LRU_BLOCKS, LRU_BW, LRU_BW), const4),
            _resident((2, 1, D_RNN), const3),
            _resident((2, 1, D_RNN), const3),
        ],
        out_specs=[
            pl.BlockSpec(blk, lambda b, s: (b, fwd(s), 0)),
            pl.BlockSpec(blk, lambda b, s: (b, bwd(s), 0)),
        ],
        out_shape=[jax.ShapeDtypeStruct((nb, t, D_RNN), F32)] * 2,
        scratch_shapes=[pltpu.VMEM((8, D_RNN), F32)],
        compiler_params=_cparams(("arbitrary", "arbitrary")),
        name="lru_scan",
    )(ux, ux, ux, ux, ux, ux, w["conv_w"], w["conv_b"], w["w_a"], w["b_a"], w["w_i"], w["b_i"], w["lam"])


def _softmax_av(q, k, v):
    s = lax.dot_general(q, k, (((1,), (1,)), ((), ())), preferred_element_type=F32)
    m = jnp.max(s, axis=-1, keepdims=True)
    p = jnp.exp2(s - m)
    l = jnp.sum(p, axis=-1, keepdims=True)
    o = jnp.dot(p.astype(BF16), v, preferred_element_type=F32)
    return o / l


MLA_HPS = 4


MLA_TQ = 256


def _mla_attn_kernel(q_ref, k_ref, v_ref, o_ref, *, hps):
    q2d = q_ref.at[0] if len(q_ref.shape) == 3 else q_ref
    for hh in range(hps):
        qs = slice(hh * 256, (hh + 1) * 256)
        vs = slice(hh * MLA_V, (hh + 1) * MLA_V)
        o_ref[:, vs] = _softmax_av(q2d[:, qs], k_ref[:, qs], v_ref[:, vs]).astype(BF16)


def _mla_attn(q, k, v):
    nb, t, _ = q.shape
    lat = t - TM
    hps_c = MLA_H
    ctx_out = pl.pallas_call(
        functools.partial(_mla_attn_kernel, hps=hps_c),
        grid=(nb,),
        in_specs=[
            pl.BlockSpec((None, TM, hps_c * 256), lambda b: (b, 0, 0)),
            pl.BlockSpec((None, TM, hps_c * 256), lambda b: (b, 0, 0)),
            pl.BlockSpec((None, TM, hps_c * MLA_V), lambda b: (b, 0, 0)),
        ],
        out_specs=pl.BlockSpec((None, TM, hps_c * MLA_V), lambda b: (b, 0, 0)),
        out_shape=jax.ShapeDtypeStruct((nb, TM, MLA_H * MLA_V), BF16),
        compiler_params=_cparams(("arbitrary",)),
        name="mla_attn_ctx",
    )(q, k, v)
    hps = MLA_HPS
    lat_out = pl.pallas_call(
        functools.partial(_mla_attn_kernel, hps=hps),
        grid=(nb, MLA_H // hps, lat // MLA_TQ),
        in_specs=[
            pl.BlockSpec((pl.Element(MLA_TQ), pl.Element(hps * 256)),
                         lambda b, h, i: (pl.multiple_of(b * t + TM + i * MLA_TQ, TM),
                                          pl.multiple_of(h * (hps * 256), 256))),
            pl.BlockSpec((None, t, hps * 256), lambda b, h, i: (b, 0, h)),
            pl.BlockSpec((None, t, hps * MLA_V), lambda b, h, i: (b, 0, h)),
        ],
        out_specs=pl.BlockSpec((None, MLA_TQ, hps * MLA_V), lambda b, h, i: (b, i, h)),
        out_shape=jax.ShapeDtypeStruct((nb, lat, MLA_H * MLA_V), BF16),
        compiler_params=_cparams(("arbitrary", "arbitrary", "arbitrary")),
        name="mla_attn",
    )(q.reshape(nb * t, MLA_H * 256), k, v)
    return ctx_out, lat_out


def _route(h2, rwh_ref, rwl_ref, rb_ref, route_ref):
    hi = h2.astype(BF16)
    lo = (h2 - hi.astype(F32)).astype(BF16)
    nt_dims = (((1,), (1,)), ((), ()))
    both = lax.dot_general(rwl_ref[...], hi, nt_dims, preferred_element_type=F32)
    logits = (both[0:N_EXP] + lax.dot_general(rwh_ref[...], lo, nt_dims, preferred_element_type=F32)
              + both[N_EXP:])
    scores = jax.nn.sigmoid(logits)
    sel = scores + rb_ref[...]
    sc = [scores[e:e + 1, :] for e in range(N_EXP)]
    se = [sel[e:e + 1, :] for e in range(N_EXP)]
    gs = []
    for g in range(N_GROUPS):
        a, b, c, d = se[4 * g:4 * g + 4]
        hi_ab, lo_ab = jnp.maximum(a, b), jnp.minimum(a, b)
        hi_cd, lo_cd = jnp.maximum(c, d), jnp.minimum(c, d)
        top1 = jnp.maximum(hi_ab, hi_cd)
        top2 = jnp.maximum(jnp.maximum(lo_ab, lo_cd), jnp.minimum(hi_ab, hi_cd))
        gs.append(top1 + top2)
    best = jnp.zeros_like(gs[0])
    best_v = gs[0]
    for g in range(1, N_GROUPS):
        upd = gs[g] > best_v
        best = jnp.where(upd, float(g), best)
        best_v = jnp.where(upd, gs[g], best_v)
    masked = [jnp.where(best == float(e // EXP_PER_GROUP), se[e], NEG) for e in range(N_EXP)]
    i1 = jnp.zeros_like(best)
    v1 = masked[0]
    s1 = sc[0]
    for e in range(1, N_EXP):
        upd = masked[e] > v1
        i1 = jnp.where(upd, float(e), i1)
        v1 = jnp.where(upd, masked[e], v1)
        s1 = jnp.where(upd, sc[e], s1)
    i2 = jnp.zeros_like(best)
    v2 = jnp.full_like(v1, -jnp.inf)
    s2 = jnp.zeros_like(s1)
    for e in range(N_EXP):
        upd = jnp.logical_and(masked[e] > v2, i1 != float(e))
        i2 = jnp.where(upd, float(e), i2)
        v2 = jnp.where(upd, masked[e], v2)
        s2 = jnp.where(upd, sc[e], s2)
    tot = s1 + s2
    route_ref[0:1, :] = i1
    route_ref[1:2, :] = i2
    route_ref[2:3, :] = s1 / tot
    route_ref[3:4, :] = s2 / tot
    route_ref[4:8, :] = jnp.zeros((4, TM), F32)


def _finish_mixer(y, x, mod_ref, g_ref, rwh_ref, rwl_ref, rb_ref, xn_ref, h2_ref, route_ref):
    xn = x + mod_ref[:, 2 * D:3 * D] * y
    xn_ref[...] = xn
    h2 = _modulate(xn, g_ref[...], mod_ref[:, 3 * D:4 * D], mod_ref[:, 4 * D:5 * D])
    h2_ref[...] = h2
    _route(h2, rwh_ref, rwl_ref, rb_ref, route_ref)


def _ab_out_kernel(hf_ref, hb_ref, gate_ref, attc_ref, attl_ref, xc_ref, xl_ref, mod_ref, g_ref, wo_ref, rwh_ref,
                   rwl_ref, rb_ref, xn_ref, h2_ref, route_ref, *, off):
    rnn = ((hf_ref[...] + hb_ref[...]) * gate_ref[...].astype(F32)).astype(BF16)
    is_ctx = pl.program_id(1) + off == 0
    att = jnp.where(is_ctx, attc_ref[...], attl_ref[...])
    y = (jnp.dot(rnn, wo_ref[0:D_RNN, :], preferred_element_type=F32)
         + jnp.dot(att, wo_ref[D_RNN:, :], preferred_element_type=F32))
    x = jnp.where(is_ctx, xc_ref[...], xl_ref[...])
    _finish_mixer(y, x, mod_ref, g_ref, rwh_ref, rwl_ref, rb_ref, xn_ref, h2_ref, route_ref)


def _na_out_kernel(att_ref, x_ref, mod_ref, g_ref, wo_ref, rwh_ref, rwl_ref, rb_ref, xn_ref, h2_ref, route_ref):
    x2d = x_ref.at[0] if len(x_ref.shape) == 3 else x_ref
    for sub in range(att_ref.shape[0] // TM):
        r = pl.ds(sub * TM, TM)
        y = jnp.dot(att_ref[r, :], wo_ref[...], preferred_element_type=F32)
        _finish_mixer(y, x2d[r, :], mod_ref, g_ref, rwh_ref, rwl_ref, rb_ref,
                      xn_ref.at[r, :], h2_ref.at[r, :], route_ref.at[:, r])


def _mixer_out(kernel, acts, x_in, nb, t, mod, g, wo, rw_hi, rw_lo, rb, off):
    nt = t // TM - off
    sub = 2 if (off == 1 and nt % 2 == 0 and kernel is _na_out_kernel) else 1
    rows = sub * TM
    out_row = lambda b, i: (b, i, 0)
    const = lambda b, i: (0, 0)
    row = lambda b, i: (b, i + off, 0)
    if kernel is _na_out_kernel:
        x_arrays = (x_in,)
        if sub == 1:
            x_specs = [pl.BlockSpec((None, TM, D), row)]
        else:
            x_specs = [pl.BlockSpec((pl.Element(1), pl.Element(rows), pl.Element(D)),
                                    lambda b, i: (b, (sub * i + off) * TM, 0))]
    else:
        x_arrays, ctx_map, lat_map = _split_x(x_in, off)
        x_specs = [pl.BlockSpec((None, TM, D), ctx_map), pl.BlockSpec((None, TM, D), lat_map)]
    def act_map(a):
        if a.shape[1] == t:
            return row
        if a.shape[1] == TM:
            return lambda b, i: (b, 0, 0)
        if off == 0:
            return lambda b, i: (b, jnp.maximum(i - 1, 0), 0)
        return out_row

    act_specs = [pl.BlockSpec((None, rows, a.shape[2]), act_map(a)) for a in acts]
    assert sub == 1 or all(a.shape[1] == t - TM for a in acts)
    return pl.pallas_call(
        kernel,
        grid=(nb, nt // sub),
        in_specs=act_specs + x_specs + [
            _mod_spec(nb, off),
            _resident((1, D), const),
            _resident(wo.shape, const),
            _resident((N_EXP, D), const),
            _resident((2 * N_EXP, D), const),
            _resident((N_EXP, 1), const),
        ],
        out_specs=[
            pl.BlockSpec((None, rows, D), out_row),
            pl.BlockSpec((None, rows, D), out_row),
            pl.BlockSpec((None, 8, rows), lambda b, i: (b, 0, i)),
        ],
        out_shape=[
            jax.ShapeDtypeStruct((nb, nt * TM, D), F32),
            jax.ShapeDtypeStruct((nb, nt * TM, D), F32),
            jax.ShapeDtypeStruct((nb, 8, nt * TM), F32),
        ],
        compiler_params=_cparams(("arbitrary", "arbitrary")),
        name="mixer_out",
    )(*acts, *x_arrays, mod, g, wo, rw_hi, rw_lo, rb)


def _na_in_kernel(x_ref, g_ref, mod_ref, w_ref, hg_ref, o_ref):
    h = _modulate(x_ref[...], g_ref[...], mod_ref[:, 0:D], mod_ref[:, D:2 * D]).astype(BF16)
    for p in range(3):
        u = jnp.dot(h, w_ref[:, p * D:(p + 1) * D], preferred_element_type=F32)
        if p == 2:
            o_ref[p] = u.astype(BF16)
            continue
        hg = hg_ref[p]
        for hh in range(NA_H):
            uh = u[:, hh * NA_DH:(hh + 1) * NA_DH]
            o_ref[p, :, hh * NA_DH:(hh + 1) * NA_DH] = _rms(uh, hg, NA_DH).astype(BF16)


def _na_in(x_all, g, mod, w_qkv, hg):
    nb, t, _ = x_all.shape
    nt = t // TM
    return pl.pallas_call(
        _na_in_kernel,
        grid=(nb, nt),
        in_specs=[
            pl.BlockSpec((None, TM, D), lambda b, i: (b, i, 0)),
            _resident((1, D), lambda b, i: (0, 0)),
            _mod_spec(nb, 0),
            _resident((D, 3 * D), lambda b, i: (0, 0)),
            _resident((3, 1, NA_DH), lambda b, i: (0, 0, 0)),
        ],
        out_specs=pl.BlockSpec((3, None, TM, D), lambda b, i: (0, b, i, 0)),
        out_shape=jax.ShapeDtypeStruct((3, nb, t, D), BF16),
        compiler_params=_cparams(("arbitrary", "arbitrary")),
        name="na_in",
    )(x_all, g, mod, w_qkv, hg)


def _na_key_base(rb, rows):
    return jnp.clip(rb * NA_QR - NA_ROWS // 2, 0, rows - NA_KR)


NA_HPS = 4


def _na_attn_kernel(var_ref, q_ref, k_ref, v_ref, bias_ref, o_ref, *, rows):
    rb = pl.program_id(2)
    start = pl.multiple_of(TM + _na_key_base(rb, rows) * GRID_W, GRID_W)
    nk = NA_KR * GRID_W
    nt_dims = (((1,), (1,)), ((), ()))
    var = var_ref[rb]
    for hh in range(NA_HPS):
        hs = slice(hh * NA_DH, (hh + 1) * NA_DH)
        q = q_ref[:, hs]
        s_c = lax.dot_general(q, k_ref[0:TM, hs], nt_dims, preferred_element_type=F32)
        s_l = (lax.dot_general(q, k_ref[pl.ds(start, nk), hs], nt_dims, preferred_element_type=F32)
               + bias_ref[hh, var])
        m = jnp.maximum(jnp.max(s_c, axis=-1, keepdims=True), jnp.max(s_l, axis=-1, keepdims=True))
        p_c = jnp.exp2(s_c - m)
        p_l = jnp.exp2(s_l - m)
        l = jnp.sum(p_c, axis=-1, keepdims=True) + jnp.sum(p_l, axis=-1, keepdims=True)
        o = (jnp.dot(p_l.astype(BF16), v_ref[pl.ds(start, nk), hs], preferred_element_type=F32)
             + jnp.dot(p_c.astype(BF16), v_ref[0:TM, hs], preferred_element_type=F32))
        o_ref[:, hs] = (o / l).astype(BF16)


def _na_attn(qkv, bias, variant):
    _, nb, t, _ = qkv.shape
    rows = (t - TM) // GRID_W
    nrb = rows // NA_QR
    nq = NA_QR * GRID_W
    assert nq == TM
    return pl.pallas_call(
        functools.partial(_na_attn_kernel, rows=rows),
        grid_spec=pltpu.PrefetchScalarGridSpec(
            num_scalar_prefetch=1,
            grid=(NA_H // NA_HPS, nb, nrb),
            in_specs=[
                pl.BlockSpec((None, None, nq, NA_HPS * NA_DH), lambda h, b, r, var: (0, b, r + 1, h)),
                pl.BlockSpec((None, None, t, NA_HPS * NA_DH), lambda h, b, r, var: (1, b, 0, h)),
                pl.BlockSpec((None, None, t, NA_HPS * NA_DH), lambda h, b, r, var: (2, b, 0, h)),
                pl.BlockSpec((NA_HPS, bias.shape[1], nq, NA_KR * GRID_W), lambda h, b, r, var: (h, 0, 0, 0)),
            ],
            out_specs=pl.BlockSpec((None, nq, NA_HPS * NA_DH), lambda h, b, r, var: (b, r, h)),
        ),
        out_shape=jax.ShapeDtypeStruct((nb, t - TM, D), BF16),
        compiler_params=_cparams(("arbitrary", "arbitrary", "arbitrary")),
        name="na_attn",
    )(variant, qkv, qkv, qkv, bias)


def _na_bias_tables(rpb, rows):
    nrb = rows // NA_QR
    rb = np.arange(nrb)
    kb = np.clip(rb * NA_QR - NA_ROWS // 2, 0, rows - NA_KR)
    r_q = rb[:, None] * NA_QR + np.arange(NA_QR)[None, :]
    r0_q = np.clip(r_q - NA_ROWS // 2, 0, rows - NA_ROWS)
    sig = np.concatenate([r_q - kb[:, None], r0_q - kb[:, None]], axis=1)
    uniq, variant = np.unique(sig, axis=0, return_inverse=True)
    variant = np.asarray(variant).reshape(-1)
    kr = np.arange(NA_KR)[None, :]
    cols = np.arange(GRID_W)
    col_start = np.clip(cols - NA_COLS // 2, 0, GRID_W - NA_COLS)
    kc = cols[None, :]
    col_ok = (kc >= col_start[:, None]) & (kc < col_start[:, None] + NA_COLS)
    nh, ndr, ndc = rpb.shape
    per = 2 * GRID_W + 1
    p = jnp.concatenate([rpb[:, :, NA_COLS - 1:], jnp.zeros((nh, ndr, per - ndc), F32), rpb[:, :, :NA_COLS - 1]], axis=2)
    toep = jnp.tile(p, (1, 1, GRID_W))[:, :, :GRID_W * (per - 1)].reshape(nh, ndr, GRID_W, per - 1)[:, :, :, :GRID_W]
    nv = len(uniq)
    sel = np.zeros((nv, NA_QR, NA_KR, ndr), np.float32)
    row_ok = np.zeros((nv, NA_QR, NA_KR), bool)
    for v, u in enumerate(uniq):
        r_abs = u[:NA_QR][:, None]
        r0 = u[NA_QR:][:, None]
        row_ok[v] = (kr >= r0) & (kr < r0 + NA_ROWS)
        drow = np.clip(kr - r_abs + NA_ROWS - 1, 0, ndr - 1)
        sel[v][np.arange(NA_QR)[:, None], np.arange(NA_KR)[None, :], drow] = row_ok[v]
    tab = jnp.einsum("vabd,hdij->hvaibj", jnp.asarray(sel), toep, precision=lax.Precision.HIGHEST)
    ok = row_ok[:, :, None, :, None] & col_ok[None, None, :, None, :]
    tab = jnp.where(ok[None], tab, NEG)
    return tab.reshape(nh, nv, NA_QR * GRID_W, NA_KR * GRID_W), jnp.asarray(variant, I32)


def _swiglu_tile(x_bf16, wg, wu, wd):
    gate = jnp.dot(x_bf16, wg, preferred_element_type=F32)
    up = jnp.dot(x_bf16, wu, preferred_element_type=F32)
    hmid = (_silu(gate) * up).astype(BF16)
    return jnp.dot(hmid, wd, preferred_element_type=F32)


def _dispatch_kernel(pos_ref, padpos_ref, h2_ref, xs_ref, zero_ref, sem, *, nt):
    tile = pl.program_id(0) * nt + pl.program_id(1)

    def pad_copy(q):
        return pltpu.make_async_copy(zero_ref.at[pl.ds(0, 1), :], xs_ref.at[pl.ds(padpos_ref[q], 1), :], sem)

    @pl.when(tile == 0)
    def _():
        zero_ref[...] = jnp.zeros_like(zero_ref)

        def start(q, c):
            pad_copy(q).start()
            return c

        def wait(q, c):
            pad_copy(q).wait()
            return c

        lax.fori_loop(0, N_EXP * TE, start, 0, unroll=16)
        lax.fori_loop(0, N_EXP * TE, wait, 0, unroll=16)

    base = tile * (2 * TM)

    def row_copy(j, kk):
        return pltpu.make_async_copy(h2_ref.at[pl.ds(j, 1), :], xs_ref.at[pl.ds(pos_ref[base + 2 * j + kk], 1), :], sem)

    for j in range(TM):
        row_copy(j, 0).start()
        row_copy(j, 1).start()
    for j in range(TM):
        row_copy(j, 0).wait()
        row_copy(j, 1).wait()


def _dispatch(h2, pos, padpos, n_rows):
    nb, t, _ = h2.shape
    nt = t // TM
    return pl.pallas_call(
        functools.partial(_dispatch_kernel, nt=nt),
        grid_spec=pltpu.PrefetchScalarGridSpec(
            num_scalar_prefetch=2,
            grid=(nb, nt),
            in_specs=[pl.BlockSpec((TM, D), lambda b, i, *_: (b * nt + i, 0))],
            out_specs=pl.BlockSpec(memory_space=pl.ANY),
            scratch_shapes=[pltpu.VMEM((8, D), F32), pltpu.SemaphoreType.DMA],
        ),
        out_shape=jax.ShapeDtypeStruct((n_rows, D), F32),
        compiler_params=_cparams(("arbitrary", "arbitrary")),
        name="moe_dispatch",
    )(pos, padpos, h2.reshape(nb * t, D))


def _expert_kernel(te_ref, tb_ref, nu_ref, x_ref, wg_ref, wu_ref, wd_ref, y_ref):
    g = pl.program_id(0)

    @pl.when(g < nu_ref[0])
    def _():
        y_ref[...] = _swiglu_tile(x_ref[...].astype(BF16), wg_ref[...], wu_ref[...], wd_ref[...])

    @pl.when(g >= nu_ref[0])
    def _():
        y_ref[...] = jnp.zeros_like(y_ref)


def _experts(xs, tile_expert, tile_block, n_used, wg, wu, wd, layer):
    n_rows = xs.shape[0]
    n_tiles = n_rows // TE
    return pl.pallas_call(
        _expert_kernel,
        grid_spec=pltpu.PrefetchScalarGridSpec(
            num_scalar_prefetch=3,
            grid=(n_tiles,),
            in_specs=[
                pl.BlockSpec((TE, D), lambda g, te, tb, nu: (tb[g], 0)),
                pl.BlockSpec((None, None, D, D_EXP), lambda g, te, tb, nu: (layer, te[g], 0, 0)),
                pl.BlockSpec((None, None, D, D_EXP), lambda g, te, tb, nu: (layer, te[g], 0, 0)),
                pl.BlockSpec((None, None, D_EXP, D), lambda g, te, tb, nu: (layer, te[g], 0, 0)),
            ],
            out_specs=pl.BlockSpec((TE, D), lambda g, te, tb, nu: (g, 0)),
        ),
        out_shape=jax.ShapeDtypeStruct((n_rows, D), F32),
        compiler_params=_cparams(("arbitrary",)),
        name="moe_experts",
    )(tile_expert, tile_block, n_used, xs, wg, wu, wd)


def _combine_kernel(pos_ref, h2_ref, xn_ref, mod_ref, wt_ref, wsg_ref, wsu_ref, wsd_ref, ys_ref, o_ref, ybuf, sem, *, nt):
    tile = pl.program_id(0) * nt + pl.program_id(1)
    base = tile * (2 * TM)

    def row_copy(j, kk):
        return pltpu.make_async_copy(ys_ref.at[pl.ds(pos_ref[base + 2 * j + kk], 1), :], ybuf.at[kk, pl.ds(j, 1), :], sem)

    for j in range(TM):
        row_copy(j, 0).start()
        row_copy(j, 1).start()
    shared = _swiglu_tile(h2_ref[...].astype(BF16), wsg_ref[...], wsu_ref[...], wsd_ref[...])
    for j in range(TM):
        row_copy(j, 0).wait()
        row_copy(j, 1).wait()
    wt = wt_ref[...]
    y = shared + wt[:, 0:1] * ybuf[0] + wt[:, 1:2] * ybuf[1]
    o_ref[...] = xn_ref[...] + mod_ref[:, 5 * D:6 * D] * y


def _combine(pos, h2, xn, mod, wt, wsg, wsu, wsd, ys, off, layer):
    nb, t, _ = h2.shape
    nt = t // TM
    row = lambda b, i, *_: (b, i, 0)
    const = lambda b, i, *_: (layer, 0, 0)
    return pl.pallas_call(
        functools.partial(_combine_kernel, nt=nt),
        grid_spec=pltpu.PrefetchScalarGridSpec(
            num_scalar_prefetch=1,
            grid=(nb, nt),
            in_specs=[
                pl.BlockSpec((None, TM, D), row),
                pl.BlockSpec((None, TM, D), row),
                pl.BlockSpec((None, 1, ADA * D), lambda b, i, *_: (jnp.where(i + off == 0, nb, b), 0, 0)),
                pl.BlockSpec((None, TM, 8), row),
                _resident((None, D, D_EXP), const),
                _resident((None, D, D_EXP), const),
                _resident((None, D_EXP, D), const),
                pl.BlockSpec(memory_space=pl.ANY),
            ],
            out_specs=pl.BlockSpec((None, TM, D), row),
            scratch_shapes=[pltpu.VMEM((2, TM, D), F32), pltpu.SemaphoreType.DMA],
        ),
        out_shape=jax.ShapeDtypeStruct((nb, t, D), F32),
        compiler_params=_cparams(("arbitrary", "arbitrary")),
        name="moe_combine",
    )(pos, h2, xn, mod, wt, wsg, wsu, wsd, ys)


def _moe_plan(route):
    nb, _, tp = route.shape
    n_tok = nb * tp
    idx = route[:, 0:2, :].astype(I32).transpose(0, 2, 1).reshape(n_tok * 2)
    wt = jnp.pad(route[:, 2:4, :].transpose(0, 2, 1), ((0, 0), (0, 0), (0, 6)))
    onehot = (idx[:, None] == jnp.arange(N_EXP, dtype=I32)[None, :]).astype(I32)
    csum = jnp.cumsum(onehot, axis=0)
    counts = csum[-1]
    rank = jnp.sum((csum - onehot) * onehot, axis=1)
    padded = ((counts + TE - 1) // TE) * TE
    ends = jnp.cumsum(padded)
    starts = ends - padded
    pos = (jnp.sum(onehot * starts[None, :], axis=1) + rank).astype(I32)
    n_tiles = (2 * n_tok) // TE + N_EXP
    n_used = (ends[-1] // TE).astype(I32)
    tile_block = jnp.minimum(jnp.arange(n_tiles, dtype=I32), n_used - 1)
    tile_expert = jnp.sum((tile_block[:, None] * TE >= ends[None, :]).astype(I32), axis=1).astype(I32)
    npads = padded - counts
    pad_ends = jnp.cumsum(npads)
    pad_starts = pad_ends - npads
    qidx = jnp.arange(N_EXP * TE, dtype=I32)
    pe = jnp.minimum(jnp.sum((qidx[:, None] >= pad_ends[None, :]).astype(I32), axis=1), N_EXP - 1)
    in_seg = starts[pe] + counts[pe] + (qidx - pad_starts[pe])
    tail = ends[-1] + (qidx - pad_ends[-1])
    padpos = jnp.where(qidx < pad_ends[-1], in_seg, tail).astype(I32)
    return pos, padpos, tile_expert, tile_block, n_used.reshape(1), wt, n_tiles * TE


def _moe(h2, xn, route, mod, w, off, layer):
    pos, padpos, tile_expert, tile_block, n_used, wt, n_rows = _moe_plan(route)
    xs = _dispatch(h2, pos, padpos, n_rows)
    ys = _experts(xs, tile_expert, tile_block, n_used, w["wg"], w["wu"], w["wd"], layer)
    return _combine(pos, h2, xn, mod, wt, w["wsg"], w["wsu"], w["wsd"], ys, off, layer)


def _rope_tables(t, s):
    half = MLA_ROPE // 2
    inv_freq = ROPE_THETA ** (-jnp.arange(0, half, 2, dtype=F32) / half)
    tt = jnp.arange(t - s)
    ang_row = (tt // GRID_W).astype(F32)[:, None] * inv_freq
    ang_col = (tt % GRID_W).astype(F32)[:, None] * inv_freq
    ang = jnp.concatenate([ang_row] * 2 + [ang_col] * 2, axis=1)
    cos = jnp.cos(ang)
    sin = jnp.sin(ang)
    sign = np.where((np.arange(MLA_ROPE) % 32) < 16, -1.0, 1.0).astype(np.float32)
    sin = sin * sign[None, :]
    cos = jnp.concatenate([jnp.ones((s, MLA_ROPE), F32), cos], axis=0)
    sin = jnp.concatenate([jnp.zeros((s, MLA_ROPE), F32), sin], axis=0)
    cos = jnp.pad(cos, ((0, 0), (0, LANE - MLA_ROPE)), constant_values=1.0)
    sin = jnp.pad(sin, ((0, 0), (0, LANE - MLA_ROPE)))
    return cos.astype(F32), sin.astype(F32)


def _ab_weights(j, t, s, ab_w_in, lru_conv_w, lru_conv_b, lru_w_a, lru_b_a, lru_w_i, lru_b_i, lru_lambda,
                mla_q_norm_g, mla_w_uq, mla_kv_norm_g, mla_w_ukv, mla_q_head_g, mla_k_head_g):
    w_in = jnp.pad(ab_w_in[j], ((0, 0), (0, AB_IN_PAD - AB_IN))).astype(BF16)
    uq = mla_w_uq[j].reshape(Q_RANK, MLA_H, MLA_QK)
    uq_n = uq[:, :, :MLA_NOPE].reshape(Q_RANK, MLA_H * LANE)
    uq_r = jnp.pad(uq[:, :, MLA_NOPE:], ((0, 0), (0, 0), (0, LANE - MLA_ROPE))).reshape(Q_RANK, MLA_H * LANE)
    ukv = mla_w_ukv[j].reshape(KV_RANK, MLA_H, MLA_NOPE + MLA_V)
    ukv_k = ukv[:, :, :MLA_NOPE].reshape(KV_RANK, MLA_H * LANE)
    ukv_v = ukv[:, :, MLA_NOPE:].reshape(KV_RANK, MLA_H * MLA_V)
    scale = MLA_QK ** -0.5 * LOG2E
    qg = mla_q_head_g[j] * scale
    kg = mla_k_head_g[j]
    pad_r = lambda g: jnp.pad(g[MLA_NOPE:], (0, LANE - MLA_ROPE))
    hg = jnp.stack([qg[:MLA_NOPE], pad_r(qg), kg[:MLA_NOPE], pad_r(kg)], axis=0)
    cos, sin = _rope_tables(t, s)
    return dict(
        w_in=w_in, qng=mla_q_norm_g[j][None], kvng=mla_kv_norm_g[j][None],
        w_uq=jnp.concatenate([uq_n, uq_r], axis=1).astype(BF16),
        w_ukv=jnp.concatenate([ukv_k, ukv_v], axis=1).astype(BF16),
        hg=hg, cos=cos, sin=sin,
        conv_w=lru_conv_w[j], conv_b=lru_conv_b[j][None],
        w_a=lru_w_a[j].astype(BF16), b_a=lru_b_a[j][:, None, :],
        w_i=lru_w_i[j].astype(BF16), b_i=lru_b_i[j][:, None, :], lam=lru_lambda[j][:, None, :],
    )


def kernel(x, c, ctx, c_ctx, ada_w, ada_b, norm_mix_g, norm_ffn_g, ab_w_in, ab_w_out, lru_conv_w, lru_conv_b, lru_w_a, lru_b_a, lru_w_i, lru_b_i, lru_lambda, mla_q_norm_g, mla_w_uq, mla_kv_norm_g, mla_w_ukv, mla_q_head_g, mla_k_head_g, na_w_qkv, na_w_out, na_q_head_g, na_k_head_g, na_rpb, router_w, router_b, moe_w_gate, moe_w_up, moe_w_down, moe_ws_gate, moe_ws_up, moe_ws_down):
    nb, seq, _ = x.shape
    s = ctx.shape[1]
    depth = ada_w.shape[0]
    assert s == TM and seq % TM == 0 and nb < 8
    t = s + seq
    rows = seq // GRID_W

    cs = jnp.zeros((8, D), F32).at[:nb].set(c).at[nb].set(c_ctx)
    mod_all = _adaln(cs, ada_w, ada_b).reshape(depth, 8, 1, ADA * D)

    rw_t = router_w.T
    rw_hi = rw_t.astype(BF16)
    rw_lo = jnp.concatenate([rw_hi, (rw_t - rw_hi.astype(F32)).astype(BF16)], axis=0)
    rb = router_b.reshape(N_EXP, 1)

    moe_w = dict(wg=moe_w_gate.astype(BF16), wu=moe_w_up.astype(BF16), wd=moe_w_down.astype(BF16),
                 wsg=moe_ws_gate.astype(BF16), wsu=moe_ws_up.astype(BF16), wsd=moe_ws_down.astype(BF16))

    x_all = (ctx, x)
    for i in range(depth):
        last = i == depth - 1
        off = 1 if last else 0
        j = i // 2
        mod = mod_all[i]
        g_mix = norm_mix_g[i][None]
        g_ffn = norm_ffn_g[i][None]
        if i % 2 == 0:
            w = _ab_weights(j, t, s, ab_w_in, lru_conv_w, lru_conv_b, lru_w_a, lru_b_a, lru_w_i, lru_b_i, lru_lambda,
                            mla_q_norm_g, mla_w_uq, mla_kv_norm_g, mla_w_ukv, mla_q_head_g, mla_k_head_g)
            ux, gate, q, k, v = _ab_in(x_all, nb, t, g_mix, mod, w)
            h_f, h_b = _lru(ux, w)
            att_c, att_l = _mla_attn(q, k, v)
            xn, h2, route = _mixer_out(functools.partial(_ab_out_kernel, off=off), (h_f, h_b, gate, att_c, att_l),
                                       x_all, nb, t, mod, g_ffn, ab_w_out[j].astype(BF16), rw_hi, rw_lo, rb, off)
        else:
            scale = NA_DH ** -0.5 * LOG2E
            hg = jnp.stack([na_q_head_g[j] * scale, na_k_head_g[j], jnp.ones((NA_DH,), F32)], axis=0)[:, None, :]
            qkv = _na_in(x_all, g_mix, mod, na_w_qkv[j].astype(BF16), hg)
            bias, variant = _na_bias_tables(na_rpb[j] * LOG2E, rows)
            att = _na_attn(qkv, bias, variant)
            assert last, "the neighbourhood layer has no context-output path"
            xn, h2, route = _mixer_out(_na_out_kernel, (att,), x_all, nb, t, mod, g_ffn, na_w_out[j].astype(BF16),
                                       rw_hi, rw_lo, rb, off)
        x_all = _moe(h2, xn, route, mod, moe_w, off, i)
    return x_all
```

```python
import functools

import numpy as np
import jax
import jax.numpy as jnp
from jax import lax
from jax.experimental import pallas as pl
from jax.experimental.pallas import tpu as pltpu

F32 = jnp.float32
BF16 = jnp.bfloat16
I32 = jnp.int32

D = 2048
ADA = 6
EPS = 1e-6
NEG = -1e30
LOG2E = 1.4426950408889634
GRID_W = 64
D_RNN = 1024
LRU_BLOCKS = 8
LRU_BW = 128
LRU_C = 8.0
MLA_H = 8
MLA_NOPE = 128
MLA_ROPE = 64
MLA_QK = 192
MLA_V = 128
Q_RANK = 512
KV_RANK = 512
ROPE_THETA = 10000.0
AB_IN = 3136
AB_IN_PAD = 3200
NA_H = 16
NA_DH = 128
NA_ROWS = 8
NA_COLS = 16
N_EXP = 16
N_GROUPS = 4
EXP_PER_GROUP = 4
D_EXP = 1408

LANE = 128
SUBLANES = 8
TM = 256
TE = 256
VMEM_LIMIT = 56 * 1024 * 1024

NA_QR = 4
NA_KR = 12


def _cparams(sem, vmem=VMEM_LIMIT):
    return pltpu.CompilerParams(dimension_semantics=sem, vmem_limit_bytes=vmem)


def _resident(shape, index_map):
    return pl.BlockSpec(shape, index_map, pipeline_mode=pl.Buffered(1))


def _rms(x, g, n):
    ms = jnp.sum(x * x, axis=-1, keepdims=True) * (1.0 / n)
    return (x * lax.rsqrt(ms + EPS)) * g


def _modulate(x, g, shift, scale):
    return _rms(x, g, D) * (1.0 + scale) + shift


def _gelu_tanh(x):
    c = 0.7978845608028654
    return x * (0.5 * (1.0 + jnp.tanh(c * (x + 0.044715 * (x * x * x)))))


def _silu(x):
    return x * jax.nn.sigmoid(x)


ADA_TN = 1536


def _adaln_kernel(cs_ref, w_ref, b_ref, o_ref):
    a = _silu(cs_ref[...]).astype(BF16)
    o_ref[...] = jnp.dot(a, w_ref[...].astype(BF16), preferred_element_type=F32) + b_ref[...]


def _adaln(cs, ada_w, ada_b):
    depth = ada_w.shape[0]
    n = ADA * D
    return pl.pallas_call(
        _adaln_kernel,
        grid=(depth, n // ADA_TN),
        in_specs=[
            pl.BlockSpec((8, D), lambda l, j: (0, 0)),
            pl.BlockSpec((None, D, ADA_TN), lambda l, j: (l, 0, j)),
            pl.BlockSpec((None, 1, ADA_TN), lambda l, j: (l, 0, j)),
        ],
        out_specs=pl.BlockSpec((None, 8, ADA_TN), lambda l, j: (l, 0, j)),
        out_shape=jax.ShapeDtypeStruct((depth, 8, n), F32),
        compiler_params=_cparams(("arbitrary", "arbitrary")),
        name="adaln",
    )(cs, ada_w, ada_b.reshape(depth, 1, n))


def _mod_spec(nb, off):
    return pl.BlockSpec((None, 1, ADA * D), lambda b, i: (jnp.where(i + off == 0, nb, b), 0, 0))


def _split_x(x_in, off):
    ctx_map = lambda b, i: (b, 0, 0)
    if isinstance(x_in, tuple):
        return x_in, ctx_map, lambda b, i: (b, jnp.maximum(i + off - 1, 0), 0)
    return (x_in, x_in), ctx_map, lambda b, i: (b, jnp.maximum(i + off, 1), 0)


def _ab_in_kernel(xc_ref, xl_ref, g_ref, mod_ref, win_ref, qng_ref, kvng_ref, wuq_ref, wukv_ref, hg_ref, cos_ref,
                  sin_ref, ux_ref, gate_ref, q_ref, k_ref, v_ref):
    x = jnp.where(pl.program_id(1) == 0, xc_ref[...], xl_ref[...])
    h = _modulate(x, g_ref[...], mod_ref[:, 0:D], mod_ref[:, D:2 * D]).astype(BF16)
    u = jnp.dot(h, win_ref[...], preferred_element_type=F32)
    ux_ref[...] = u[:, 0:D_RNN].astype(BF16)
    gate_ref[...] = _gelu_tanh(u[:, D_RNN:2 * D_RNN]).astype(BF16)
    o2 = 2 * D_RNN
    cq = _rms(u[:, o2:o2 + Q_RANK], qng_ref[...], Q_RANK).astype(BF16)
    ckv = _rms(u[:, o2 + Q_RANK:o2 + Q_RANK + KV_RANK], kvng_ref[...], KV_RANK).astype(BF16)
    qf = jnp.dot(cq, wuq_ref[...], preferred_element_type=F32)
    kvf = jnp.dot(ckv, wukv_ref[...], preferred_element_type=F32)
    kr = u[:, o2 + Q_RANK + KV_RANK:AB_IN_PAD]
    cos = cos_ref[...]
    sin = sin_ref[...]
    lane = lax.broadcasted_iota(I32, (TM, LANE), 1)
    first_half = (lane % 32) < 16

    def rope(z):
        partner = jnp.where(first_half, pltpu.roll(z, LANE - 16, 1), pltpu.roll(z, 16, 1))
        return z * cos + partner * sin

    hg = hg_ref[...]
    kr_ss = jnp.sum(kr * kr, axis=-1, keepdims=True)
    nh = MLA_H * MLA_NOPE
    for hh in range(MLA_H):
        qn = qf[:, hh * LANE:(hh + 1) * LANE]
        qr = qf[:, nh + hh * LANE:nh + (hh + 1) * LANE]
        ss = jnp.sum(qn * qn, axis=-1, keepdims=True) + jnp.sum(qr * qr, axis=-1, keepdims=True)
        inv = lax.rsqrt(ss * (1.0 / MLA_QK) + EPS)
        q_ref[:, hh * 256:hh * 256 + LANE] = ((qn * inv) * hg[0:1]).astype(BF16)
        q_ref[:, hh * 256 + LANE:(hh + 1) * 256] = rope((qr * inv) * hg[1:2]).astype(BF16)
        kn = kvf[:, hh * LANE:(hh + 1) * LANE]
        ss = jnp.sum(kn * kn, axis=-1, keepdims=True) + kr_ss
        inv = lax.rsqrt(ss * (1.0 / MLA_QK) + EPS)
        k_ref[:, hh * 256:hh * 256 + LANE] = ((kn * inv) * hg[2:3]).astype(BF16)
        k_ref[:, hh * 256 + LANE:(hh + 1) * 256] = rope((kr * inv) * hg[3:4]).astype(BF16)
    v_ref[...] = kvf[:, nh:].astype(BF16)


def _ab_in(x_in, nb, t, g, mod, w):
    nt = t // TM
    (x_c, x_l), ctx_map, lat_map = _split_x(x_in, 0)
    row = lambda b, i: (b, i, 0)
    const = lambda b, i: (0, 0)
    outs = [
        jax.ShapeDtypeStruct((nb, t, D_RNN), BF16),
        jax.ShapeDtypeStruct((nb, t, D_RNN), BF16),
        jax.ShapeDtypeStruct((nb, t, MLA_H * 256), BF16),
        jax.ShapeDtypeStruct((nb, t, MLA_H * 256), BF16),
        jax.ShapeDtypeStruct((nb, t, MLA_H * MLA_V), BF16),
    ]
    return pl.pallas_call(
        _ab_in_kernel,
        grid=(nb, nt),
        in_specs=[
            pl.BlockSpec((None, TM, D), ctx_map),
            pl.BlockSpec((None, TM, D), lat_map),
            _resident((1, D), const),
            _mod_spec(nb, 0),
            _resident((D, AB_IN_PAD), const),
            _resident((1, Q_RANK), const),
            _resident((1, KV_RANK), const),
            _resident((Q_RANK, 2 * MLA_H * LANE), const),
            _resident((KV_RANK, 2 * MLA_H * LANE), const),
            _resident((4, LANE), const),
            pl.BlockSpec((TM, LANE), lambda b, i: (i, 0)),
            pl.BlockSpec((TM, LANE), lambda b, i: (i, 0)),
        ],
        out_specs=[
            pl.BlockSpec((None, TM, D_RNN), row),
            pl.BlockSpec((None, TM, D_RNN), row),
            pl.BlockSpec((None, TM, MLA_H * 256), row),
            pl.BlockSpec((None, TM, MLA_H * 256), row),
            pl.BlockSpec((None, TM, MLA_H * MLA_V), row),
        ],
        out_shape=outs,
        compiler_params=_cparams(("arbitrary", "arbitrary")),
        name="ab_in",
    )(x_c, x_l, g, mod, w["w_in"], w["qng"], w["kvng"], w["w_uq"], w["w_ukv"], w["hg"], w["cos"], w["sin"])


def _lru_dir(prev_ref, cur_ref, next_ref, c, n_chunks, d, reverse,
             cw_ref, cb_ref, wa_ref, ba_ref, wi_ref, bi_ref, lam_ref, carry_ref, out_ref):
    xc = cur_ref[...].astype(F32)
    row = lax.broadcasted_iota(I32, (TM, 1), 0)
    left_ok = jnp.where(c >= 2, 1.0, 0.0)
    right_ok = jnp.where(jnp.logical_and(c >= 1, c <= n_chunks - 2), 1.0, 0.0)
    p0 = prev_ref[TM - 2:TM - 1, :].astype(F32) * left_ok
    p1 = prev_ref[TM - 1:TM, :].astype(F32) * left_ok
    n0 = next_ref[0:1, :].astype(F32) * right_ok
    x_m2 = jnp.where(row >= 2, pltpu.roll(xc, 2, 0), jnp.where(row == 0, p0, p1))
    x_m1 = jnp.where(row >= 1, pltpu.roll(xc, 1, 0), p1)
    x_p1 = jnp.where(row <= TM - 2, pltpu.roll(xc, TM - 1, 0), n0)
    cw = cw_ref[...]
    xconv = x_m2 * cw[0:1] + x_m1 * cw[1:2] + xc * cw[2:3] + x_p1 * cw[3:4] + cb_ref[...]

    xb = xconv.astype(BF16)
    ra = []
    ri = []
    for n in range(LRU_BLOCKS):
        xs = xb[:, n * LRU_BW:(n + 1) * LRU_BW]
        ra.append(jnp.dot(xs, wa_ref[d, n], preferred_element_type=F32))
        ri.append(jnp.dot(xs, wi_ref[d, n], preferred_element_type=F32))
    r = jax.nn.sigmoid(jnp.concatenate(ra, axis=1) + ba_ref[d])
    gi = jax.nn.sigmoid(jnp.concatenate(ri, axis=1) + bi_ref[d])
    neg_lam = -lam_ref[d]
    softplus = jnp.maximum(neg_lam, 0.0) + jnp.log1p(jnp.exp(-jnp.abs(neg_lam)))
    log_a = (-LRU_C * r) * softplus
    a = jnp.exp(log_a)
    th = jnp.tanh(log_a)
    neg_expm1 = (-2.0 * th) / (1.0 - th)
    bb = jnp.sqrt(neg_expm1) * gi * xconv

    ng = TM // SUBLANES
    a3 = a.reshape(ng, SUBLANES, D_RNN)
    b3 = bb.reshape(ng, SUBLANES, D_RNN)
    sub = lax.broadcasted_iota(I32, (1, SUBLANES, 1), 1)
    k = 1
    while k < SUBLANES:
        if reverse:
            keep = sub < SUBLANES - k
            a_s = jnp.where(keep, pltpu.roll(a3, SUBLANES - k, 1), 1.0)
            b_s = jnp.where(keep, pltpu.roll(b3, SUBLANES - k, 1), 0.0)
        else:
            keep = sub >= k
            a_s = jnp.where(keep, pltpu.roll(a3, k, 1), 1.0)
            b_s = jnp.where(keep, pltpu.roll(b3, k, 1), 0.0)
        b3 = b3 + a3 * b_s
        a3 = a3 * a_s
        k *= 2
    hcar = carry_ref[d:d + 1, :]
    last = 0 if reverse else SUBLANES - 1
    for g in (range(ng - 1, -1, -1) if reverse else range(ng)):
        h = b3[g] + a3[g] * hcar
        out_ref[g * SUBLANES:(g + 1) * SUBLANES, :] = h
        hcar = h[last:last + 1, :]
    carry_ref[d:d + 1, :] = hcar


def _lru_kernel(pf_ref, cf_ref, nf_ref, pb_ref, cbk_ref, nb_ref, cw_ref, cb_ref, wa_ref, ba_ref, wi_ref, bi_ref, lam_ref,
                hf_ref, hb_ref, carry_ref, *, n_chunks):
    s = pl.program_id(1)

    @pl.when(s == 0)
    def _():
        carry_ref[...] = jnp.zeros_like(carry_ref)

    params = (cw_ref, cb_ref, wa_ref, ba_ref, wi_ref, bi_ref, lam_ref, carry_ref)
    _lru_dir(pf_ref, cf_ref, nf_ref, s, n_chunks, 0, False, *params, hf_ref)
    cb_idx = jnp.where(s == 0, 0, n_chunks - s)
    _lru_dir(pb_ref, cbk_ref, nb_ref, cb_idx, n_chunks, 1, True, *params, hb_ref)


def _lru(ux, w):
    nb, t, _ = ux.shape
    nt = t // TM
    fwd = lambda s: s
    bwd = lambda s: jnp.where(s == 0, 0, nt - s)
    prev = lambda c: jnp.maximum(c - 1, 0)
    nxt = lambda c: jnp.minimum(c + 1, nt - 1)
    blk = (None, TM, D_RNN)
    const2 = lambda b, s: (0, 0)
    const3 = lambda b, s: (0, 0, 0)
    const4 = lambda b, s: (0, 0, 0, 0)
    return pl.pallas_call(
        functools.partial(_lru_kernel, n_chunks=nt),
        grid=(nb, nt),
        in_specs=[
            pl.BlockSpec(blk, lambda b, s: (b, prev(fwd(s)), 0)),
            pl.BlockSpec(blk, lambda b, s: (b, fwd(s), 0)),
            pl.BlockSpec(blk, lambda b, s: (b, nxt(fwd(s)), 0)),
            pl.BlockSpec(blk, lambda b, s: (b, prev(bwd(s)), 0)),
            pl.BlockSpec(blk, lambda b, s: (b, bwd(s), 0)),
            pl.BlockSpec(blk, lambda b, s: (b, nxt(bwd(s)), 0)),
            _resident((4, D_RNN), const2),
            _resident((1, D_RNN), const2),
            _resident((2, LRU_BLOCKS, LRU_BW, LRU_BW), const4),
            _resident((2, 1, D_RNN), const3),
            _resident((2, LRU_BLOCKS, LRU_BW, LRU_BW), const4),
            _resident((2, 1, D_RNN), const3),
            _resident((2, 1, D_RNN), const3),
        ],
        out_specs=[
            pl.BlockSpec(blk, lambda b, s: (b, fwd(s), 0)),
            pl.BlockSpec(blk, lambda b, s: (b, bwd(s), 0)),
        ],
        out_shape=[jax.ShapeDtypeStruct((nb, t, D_RNN), F32)] * 2,
        scratch_shapes=[pltpu.VMEM((8, D_RNN), F32)],
        compiler_params=_cparams(("arbitrary", "arbitrary")),
        name="lru_scan",
    )(ux, ux, ux, ux, ux, ux, w["conv_w"], w["conv_b"], w["w_a"], w["b_a"], w["w_i"], w["b_i"], w["lam"])


def _softmax_av(q, k, v):
    s = lax.dot_general(q, k, (((1,), (1,)), ((), ())), preferred_element_type=F32)
    m = jnp.max(s, axis=-1, keepdims=True)
    p = jnp.exp2(s - m)
    l = jnp.sum(p, axis=-1, keepdims=True)
    o = jnp.dot(p.astype(BF16), v, preferred_element_type=F32)
    return o / l


MLA_HPS = 4


MLA_TQ = 256


def _mla_attn_kernel(q_ref, k_ref, v_ref, o_ref, *, hps):
    q2d = q_ref.at[0] if len(q_ref.shape) == 3 else q_ref
    for hh in range(hps):
        qs = slice(hh * 256, (hh + 1) * 256)
        vs = slice(hh * MLA_V, (hh + 1) * MLA_V)
        o_ref[:, vs] = _softmax_av(q2d[:, qs], k_ref[:, qs], v_ref[:, vs]).astype(BF16)


def _mla_attn(q, k, v):
    nb, t, _ = q.shape
    lat = t - TM
    hps_c = MLA_H
    ctx_out = pl.pallas_call(
        functools.partial(_mla_attn_kernel, hps=hps_c),
        grid=(nb,),
        in_specs=[
            pl.BlockSpec((None, TM, hps_c * 256), lambda b: (b, 0, 0)),
            pl.BlockSpec((None, TM, hps_c * 256), lambda b: (b, 0, 0)),
            pl.BlockSpec((None, TM, hps_c * MLA_V), lambda b: (b, 0, 0)),
        ],
        out_specs=pl.BlockSpec((None, TM, hps_c * MLA_V), lambda b: (b, 0, 0)),
        out_shape=jax.ShapeDtypeStruct((nb, TM, MLA_H * MLA_V), BF16),
        compiler_params=_cparams(("arbitrary",)),
        name="mla_attn_ctx",
    )(q, k, v)
    hps = MLA_HPS
    lat_out = pl.pallas_call(
        functools.partial(_mla_attn_kernel, hps=hps),
        grid=(nb, MLA_H // hps, lat // MLA_TQ),
        in_specs=[
            pl.BlockSpec((pl.Element(MLA_TQ), pl.Element(hps * 256)),
                         lambda b, h, i: (pl.multiple_of(b * t + TM + i * MLA_TQ, TM),
                                          pl.multiple_of(h * (hps * 256), 256))),
            pl.BlockSpec((None, t, hps * 256), lambda b, h, i: (b, 0, h)),
            pl.BlockSpec((None, t, hps * MLA_V), lambda b, h, i: (b, 0, h)),
        ],
        out_specs=pl.BlockSpec((None, MLA_TQ, hps * MLA_V), lambda b, h, i: (b, i, h)),
        out_shape=jax.ShapeDtypeStruct((nb, lat, MLA_H * MLA_V), BF16),
        compiler_params=_cparams(("arbitrary", "arbitrary", "arbitrary")),
        name="mla_attn",
    )(q.reshape(nb * t, MLA_H * 256), k, v)
    return ctx_out, lat_out


def _route(h2, rwh_ref, rwl_ref, rb_ref, route_ref):
    hi = h2.astype(BF16)
    lo = (h2 - hi.astype(F32)).astype(BF16)
    nt_dims = (((1,), (1,)), ((), ()))
    both = lax.dot_general(rwl_ref[...], hi, nt_dims, preferred_element_type=F32)
    logits = (both[0:N_EXP] + lax.dot_general(rwh_ref[...], lo, nt_dims, preferred_element_type=F32)
              + both[N_EXP:])
    scores = jax.nn.sigmoid(logits)
    sel = scores + rb_ref[...]
    sc = [scores[e:e + 1, :] for e in range(N_EXP)]
    se = [sel[e:e + 1, :] for e in range(N_EXP)]
    gs = []
    for g in range(N_GROUPS):
        a, b, c, d = se[4 * g:4 * g + 4]
        hi_ab, lo_ab = jnp.maximum(a, b), jnp.minimum(a, b)
        hi_cd, lo_cd = jnp.maximum(c, d), jnp.minimum(c, d)
        top1 = jnp.maximum(hi_ab, hi_cd)
        top2 = jnp.maximum(jnp.maximum(lo_ab, lo_cd), jnp.minimum(hi_ab, hi_cd))
        gs.append(top1 + top2)
    best = jnp.zeros_like(gs[0])
    best_v = gs[0]
    for g in range(1, N_GROUPS):
        upd = gs[g] > best_v
        best = jnp.where(upd, float(g), best)
        best_v = jnp.where(upd, gs[g], best_v)
    masked = [jnp.where(best == float(e // EXP_PER_GROUP), se[e], NEG) for e in range(N_EXP)]
    i1 = jnp.zeros_like(best)
    v1 = masked[0]
    s1 = sc[0]
    for e in range(1, N_EXP):
        upd = masked[e] > v1
        i1 = jnp.where(upd, float(e), i1)
        v1 = jnp.where(upd, masked[e], v1)
        s1 = jnp.where(upd, sc[e], s1)
    i2 = jnp.zeros_like(best)
    v2 = jnp.full_like(v1, -jnp.inf)
    s2 = jnp.zeros_like(s1)
    for e in range(N_EXP):
        upd = jnp.logical_and(masked[e] > v2, i1 != float(e))
        i2 = jnp.where(upd, float(e), i2)
        v2 = jnp.where(upd, masked[e], v2)
        s2 = jnp.where(upd, sc[e], s2)
    tot = s1 + s2
    route_ref[0:1, :] = i1
    route_ref[1:2, :] = i2
    route_ref[2:3, :] = s1 / tot
    route_ref[3:4, :] = s2 / tot
    route_ref[4:8, :] = jnp.zeros((4, TM), F32)


def _finish_mixer(y, x, mod_ref, g_ref, rwh_ref, rwl_ref, rb_ref, xn_ref, h2_ref, route_ref):
    xn = x + mod_ref[:, 2 * D:3 * D] * y
    xn_ref[...] = xn
    h2 = _modulate(xn, g_ref[...], mod_ref[:, 3 * D:4 * D], mod_ref[:, 4 * D:5 * D])
    h2_ref[...] = h2
    _route(h2, rwh_ref, rwl_ref, rb_ref, route_ref)


def _ab_out_kernel(hf_ref, hb_ref, gate_ref, attc_ref, attl_ref, xc_ref, xl_ref, mod_ref, g_ref, wo_ref, rwh_ref,
                   rwl_ref, rb_ref, xn_ref, h2_ref, route_ref, *, off):
    rnn = ((hf_ref[...] + hb_ref[...]) * gate_ref[...].astype(F32)).astype(BF16)
    is_ctx = pl.program_id(1) + off == 0
    att = jnp.where(is_ctx, attc_ref[...], attl_ref[...])
    y = (jnp.dot(rnn, wo_ref[0:D_RNN, :], preferred_element_type=F32)
         + jnp.dot(att, wo_ref[D_RNN:, :], preferred_element_type=F32))
    x = jnp.where(is_ctx, xc_ref[...], xl_ref[...])
    _finish_mixer(y, x, mod_ref, g_ref, rwh_ref, rwl_ref, rb_ref, xn_ref, h2_ref, route_ref)


def _na_out_kernel(att_ref, x_ref, mod_ref, g_ref, wo_ref, rwh_ref, rwl_ref, rb_ref, xn_ref, h2_ref, route_ref):
    x2d = x_ref.at[0] if len(x_ref.shape) == 3 else x_ref
    for sub in range(att_ref.shape[0] // TM):
        r = pl.ds(sub * TM, TM)
        y = jnp.dot(att_ref[r, :], wo_ref[...], preferred_element_type=F32)
        _finish_mixer(y, x2d[r, :], mod_ref, g_ref, rwh_ref, rwl_ref, rb_ref,
                      xn_ref.at[r, :], h2_ref.at[r, :], route_ref.at[:, r])


def _mixer_out(kernel, acts, x_in, nb, t, mod, g, wo, rw_hi, rw_lo, rb, off):
    nt = t // TM - off
    sub = 2 if (off == 1 and nt % 2 == 0 and kernel is _na_out_kernel) else 1
    rows = sub * TM
    out_row = lambda b, i: (b, i, 0)
    const = lambda b, i: (0, 0)
    row = lambda b, i: (b, i + off, 0)
    if kernel is _na_out_kernel:
        x_arrays = (x_in,)
        if sub == 1:
            x_specs = [pl.BlockSpec((None, TM, D), row)]
        else:
            x_specs = [pl.BlockSpec((pl.Element(1), pl.Element(rows), pl.Element(D)),
                                    lambda b, i: (b, (sub * i + off) * TM, 0))]
    else:
        x_arrays, ctx_map, lat_map = _split_x(x_in, off)
        x_specs = [pl.BlockSpec((None, TM, D), ctx_map), pl.BlockSpec((None, TM, D), lat_map)]
    def act_map(a):
        if a.shape[1] == t:
            return row
        if a.shape[1] == TM:
            return lambda b, i: (b, 0, 0)
        if off == 0:
            return lambda b, i: (b, jnp.maximum(i - 1, 0), 0)
        return out_row

    act_specs = [pl.BlockSpec((None, rows, a.shape[2]), act_map(a)) for a in acts]
    assert sub == 1 or all(a.shape[1] == t - TM for a in acts)
    return pl.pallas_call(
        kernel,
        grid=(nb, nt // sub),
        in_specs=act_specs + x_specs + [
            _mod_spec(nb, off),
            _resident((1, D), const),
            _resident(wo.shape, const),
            _resident((N_EXP, D), const),
            _resident((2 * N_EXP, D), const),
            _resident((N_EXP, 1), const),
        ],
        out_specs=[
            pl.BlockSpec((None, rows, D), out_row),
            pl.BlockSpec((None, rows, D), out_row),
            pl.BlockSpec((None, 8, rows), lambda b, i: (b, 0, i)),
        ],
        out_shape=[
            jax.ShapeDtypeStruct((nb, nt * TM, D), F32),
            jax.ShapeDtypeStruct((nb, nt * TM, D), F32),
            jax.ShapeDtypeStruct((nb, 8, nt * TM), F32),
        ],
        compiler_params=_cparams(("arbitrary", "arbitrary")),
        name="mixer_out",
    )(*acts, *x_arrays, mod, g, wo, rw_hi, rw_lo, rb)


def _na_in_kernel(x_ref, g_ref, mod_ref, w_ref, hg_ref, o_ref):
    h = _modulate(x_ref[...], g_ref[...], mod_ref[:, 0:D], mod_ref[:, D:2 * D]).astype(BF16)
    for p in range(3):
        u = jnp.dot(h, w_ref[:, p * D:(p + 1) * D], preferred_element_type=F32)
        if p == 2:
            o_ref[p] = u.astype(BF16)
            continue
        hg = hg_ref[p]
        for hh in range(NA_H):
            uh = u[:, hh * NA_DH:(hh + 1) * NA_DH]
            o_ref[p, :, hh * NA_DH:(hh + 1) * NA_DH] = _rms(uh, hg, NA_DH).astype(BF16)


def _na_in(x_all, g, mod, w_qkv, hg):
    nb, t, _ = x_all.shape
    nt = t // TM
    return pl.pallas_call(
        _na_in_kernel,
        grid=(nb, nt),
        in_specs=[
            pl.BlockSpec((None, TM, D), lambda b, i: (b, i, 0)),
            _resident((1, D), lambda b, i: (0, 0)),
            _mod_spec(nb, 0),
            _resident((D, 3 * D), lambda b, i: (0, 0)),
            _resident((3, 1, NA_DH), lambda b, i: (0, 0, 0)),
        ],
        out_specs=pl.BlockSpec((3, None, TM, D), lambda b, i: (0, b, i, 0)),
        out_shape=jax.ShapeDtypeStruct((3, nb, t, D), BF16),
        compiler_params=_cparams(("arbitrary", "arbitrary")),
        name="na_in",
    )(x_all, g, mod, w_qkv, hg)


def _na_key_base(rb, rows):
    return jnp.clip(rb * NA_QR - NA_ROWS // 2, 0, rows - NA_KR)


NA_HPS = 4


NA_RPS = 2


def _na_attn_kernel(var_ref, q_ref, k_ref, v_ref, bias_ref, o_ref, *, rows):
    nk = NA_KR * GRID_W
    nt_dims = (((1,), (1,)), ((), ()))
    for sb in range(NA_RPS):
        rb = pl.program_id(2) * NA_RPS + sb
        start = pl.multiple_of(TM + _na_key_base(rb, rows) * GRID_W, GRID_W)
        var = var_ref[rb]
        qrows = slice(sb * TM, (sb + 1) * TM)
        for hh in range(NA_HPS):
            hs = slice(hh * NA_DH, (hh + 1) * NA_DH)
            q = q_ref[qrows, hs]
            s_c = lax.dot_general(q, k_ref[0:TM, hs], nt_dims, preferred_element_type=F32)
            s_l = (lax.dot_general(q, k_ref[pl.ds(start, nk), hs], nt_dims, preferred_element_type=F32)
                   + bias_ref[hh, var])
            m = jnp.maximum(jnp.max(s_c, axis=-1, keepdims=True), jnp.max(s_l, axis=-1, keepdims=True))
            p_c = jnp.exp2(s_c - m)
            p_l = jnp.exp2(s_l - m)
            l = jnp.sum(p_c, axis=-1, keepdims=True) + jnp.sum(p_l, axis=-1, keepdims=True)
            o = (jnp.dot(p_l.astype(BF16), v_ref[pl.ds(start, nk), hs], preferred_element_type=F32)
                 + jnp.dot(p_c.astype(BF16), v_ref[0:TM, hs], preferred_element_type=F32))
            o_ref[qrows, hs] = (o / l).astype(BF16)


def _na_attn(qkv, bias, variant):
    _, nb, t, _ = qkv.shape
    rows = (t - TM) // GRID_W
    nrb = rows // NA_QR
    nq = NA_QR * GRID_W
    assert nq == TM
    return pl.pallas_call(
        functools.partial(_na_attn_kernel, rows=rows),
        grid_spec=pltpu.PrefetchScalarGridSpec(
            num_scalar_prefetch=1,
            grid=(NA_H // NA_HPS, nb, nrb // NA_RPS),
            in_specs=[
                pl.BlockSpec((pl.Element(NA_RPS * nq), pl.Element(NA_HPS * NA_DH)),
                             lambda h, b, r, var: (pl.multiple_of(b * t + TM + r * (NA_RPS * nq), TM),
                                                   pl.multiple_of(h * (NA_HPS * NA_DH), NA_HPS * NA_DH))),
                pl.BlockSpec((None, None, t, NA_HPS * NA_DH), lambda h, b, r, var: (1, b, 0, h)),
                pl.BlockSpec((None, None, t, NA_HPS * NA_DH), lambda h, b, r, var: (2, b, 0, h)),
                pl.BlockSpec((NA_HPS, bias.shape[1], nq, NA_KR * GRID_W), lambda h, b, r, var: (h, 0, 0, 0)),
            ],
            out_specs=pl.BlockSpec((None, NA_RPS * nq, NA_HPS * NA_DH), lambda h, b, r, var: (b, r, h)),
        ),
        out_shape=jax.ShapeDtypeStruct((nb, t - TM, D), BF16),
        compiler_params=_cparams(("arbitrary", "arbitrary", "arbitrary")),
        name="na_attn",
    )(variant, qkv.reshape(3 * nb * t, D), qkv, qkv, bias)


def _na_bias_tables(rpb, rows):
    nrb = rows // NA_QR
    rb = np.arange(nrb)
    kb = np.clip(rb * NA_QR - NA_ROWS // 2, 0, rows - NA_KR)
    r_q = rb[:, None] * NA_QR + np.arange(NA_QR)[None, :]
    r0_q = np.clip(r_q - NA_ROWS // 2, 0, rows - NA_ROWS)
    sig = np.concatenate([r_q - kb[:, None], r0_q - kb[:, None]], axis=1)
    uniq, variant = np.unique(sig, axis=0, return_inverse=True)
    variant = np.asarray(variant).reshape(-1)
    kr = np.arange(NA_KR)[None, :]
    cols = np.arange(GRID_W)
    col_start = np.clip(cols - NA_COLS // 2, 0, GRID_W - NA_COLS)
    kc = cols[None, :]
    col_ok = (kc >= col_start[:, None]) & (kc < col_start[:, None] + NA_COLS)
    nh, ndr, ndc = rpb.shape
    per = 2 * GRID_W + 1
    p = jnp.concatenate([rpb[:, :, NA_COLS - 1:], jnp.zeros((nh, ndr, per - ndc), F32), rpb[:, :, :NA_COLS - 1]], axis=2)
    toep = jnp.tile(p, (1, 1, GRID_W))[:, :, :GRID_W * (per - 1)].reshape(nh, ndr, GRID_W, per - 1)[:, :, :, :GRID_W]
    nv = len(uniq)
    sel = np.zeros((nv, NA_QR, NA_KR, ndr), np.float32)
    row_ok = np.zeros((nv, NA_QR, NA_KR), bool)
    for v, u in enumerate(uniq):
        r_abs = u[:NA_QR][:, None]
        r0 = u[NA_QR:][:, None]
        row_ok[v] = (kr >= r0) & (kr < r0 + NA_ROWS)
        drow = np.clip(kr - r_abs + NA_ROWS - 1, 0, ndr - 1)
        sel[v][np.arange(NA_QR)[:, None], np.arange(NA_KR)[None, :], drow] = row_ok[v]
    tab = jnp.einsum("vabd,hdij->hvaibj", jnp.asarray(sel), toep, precision=lax.Precision.HIGHEST)
    ok = row_ok[:, :, None, :, None] & col_ok[None, None, :, None, :]
    tab = jnp.where(ok[None], tab, NEG)
    return tab.reshape(nh, nv, NA_QR * GRID_W, NA_KR * GRID_W), jnp.asarray(variant, I32)


def _swiglu_tile(x_bf16, wg, wu, wd):
    gate = jnp.dot(x_bf16, wg, preferred_element_type=F32)
    up = jnp.dot(x_bf16, wu, preferred_element_type=F32)
    hmid = (_silu(gate) * up).astype(BF16)
    return jnp.dot(hmid, wd, preferred_element_type=F32)


def _dispatch_kernel(pos_ref, padpos_ref, h2_ref, wsg_ref, wsu_ref, xs_ref, hmid_ref, zero_ref, sem, *, nt):
    tile = pl.program_id(0) * nt + pl.program_id(1)

    def pad_copy(q):
        return pltpu.make_async_copy(zero_ref.at[pl.ds(0, 1), :], xs_ref.at[pl.ds(padpos_ref[q], 1), :], sem)

    @pl.when(tile == 0)
    def _():
        zero_ref[...] = jnp.zeros_like(zero_ref)

        def start(q, c):
            pad_copy(q).start()
            return c

        def wait(q, c):
            pad_copy(q).wait()
            return c

        lax.fori_loop(0, N_EXP * TE, start, 0, unroll=16)
        lax.fori_loop(0, N_EXP * TE, wait, 0, unroll=16)

    base = tile * (2 * TM)

    def row_copy(j, kk):
        return pltpu.make_async_copy(h2_ref.at[pl.ds(j, 1), :], xs_ref.at[pl.ds(pos_ref[base + 2 * j + kk], 1), :], sem)

    for j in range(TM):
        row_copy(j, 0).start()
        row_copy(j, 1).start()
    hb = h2_ref[...].astype(BF16)
    gate = jnp.dot(hb, wsg_ref[...], preferred_element_type=F32)
    up = jnp.dot(hb, wsu_ref[...], preferred_element_type=F32)
    hmid_ref[...] = (_silu(gate) * up).astype(BF16)
    for j in range(TM):
        row_copy(j, 0).wait()
        row_copy(j, 1).wait()


def _dispatch(h2, pos, padpos, n_rows, wsg, wsu, layer):
    nb, t, _ = h2.shape
    nt = t // TM
    const = lambda b, i, *_: (layer, 0, 0)
    return pl.pallas_call(
        functools.partial(_dispatch_kernel, nt=nt),
        grid_spec=pltpu.PrefetchScalarGridSpec(
            num_scalar_prefetch=2,
            grid=(nb, nt),
            in_specs=[
                pl.BlockSpec((TM, D), lambda b, i, *_: (b * nt + i, 0)),
                _resident((None, D, D_EXP), const),
                _resident((None, D, D_EXP), const),
            ],
            out_specs=[
                pl.BlockSpec(memory_space=pl.ANY),
                pl.BlockSpec((TM, D_EXP), lambda b, i, *_: (b * nt + i, 0)),
            ],
            scratch_shapes=[pltpu.VMEM((8, D), F32), pltpu.SemaphoreType.DMA],
        ),
        out_shape=[jax.ShapeDtypeStruct((n_rows, D), F32), jax.ShapeDtypeStruct((nb * t, D_EXP), BF16)],
        compiler_params=_cparams(("arbitrary", "arbitrary")),
        name="moe_dispatch",
    )(pos, padpos, h2.reshape(nb * t, D), wsg, wsu)


def _expert_kernel(te_ref, tb_ref, nu_ref, x_ref, wg_ref, wu_ref, wd_ref, y_ref):
    g = pl.program_id(0)

    @pl.when(g < nu_ref[0])
    def _():
        y_ref[...] = _swiglu_tile(x_ref[...].astype(BF16), wg_ref[...], wu_ref[...], wd_ref[...])

    @pl.when(g >= nu_ref[0])
    def _():
        y_ref[...] = jnp.zeros_like(y_ref)


def _experts(xs, tile_expert, tile_block, n_used, wg, wu, wd, layer):
    n_rows = xs.shape[0]
    n_tiles = n_rows // TE
    return pl.pallas_call(
        _expert_kernel,
        grid_spec=pltpu.PrefetchScalarGridSpec(
            num_scalar_prefetch=3,
            grid=(n_tiles,),
            in_specs=[
                pl.BlockSpec((TE, D), lambda g, te, tb, nu: (tb[g], 0)),
                pl.BlockSpec((None, None, D, D_EXP), lambda g, te, tb, nu: (layer, te[g], 0, 0)),
                pl.BlockSpec((None, None, D, D_EXP), lambda g, te, tb, nu: (layer, te[g], 0, 0)),
                pl.BlockSpec((None, None, D_EXP, D), lambda g, te, tb, nu: (layer, te[g], 0, 0)),
            ],
            out_specs=pl.BlockSpec((TE, D), lambda g, te, tb, nu: (g, 0)),
        ),
        out_shape=jax.ShapeDtypeStruct((n_rows, D), F32),
        compiler_params=_cparams(("arbitrary",)),
        name="moe_experts",
    )(tile_expert, tile_block, n_used, xs, wg, wu, wd)


def _combine_kernel(pos_ref, hmid_ref, xn_ref, mod_ref, wt_ref, wsd_ref, ys_ref, o_ref, ybuf, sem, *, nt):
    tile = pl.program_id(0) * nt + pl.program_id(1)
    base = tile * (2 * TM)

    def row_copy(j, kk):
        return pltpu.make_async_copy(ys_ref.at[pl.ds(pos_ref[base + 2 * j + kk], 1), :], ybuf.at[kk, pl.ds(j, 1), :], sem)

    for j in range(TM):
        row_copy(j, 0).start()
        row_copy(j, 1).start()
    shared = jnp.dot(hmid_ref[...], wsd_ref[...], preferred_element_type=F32)
    for j in range(TM):
        row_copy(j, 0).wait()
        row_copy(j, 1).wait()
    wt = wt_ref[...]
    y = shared + wt[:, 0:1] * ybuf[0] + wt[:, 1:2] * ybuf[1]
    o_ref[...] = xn_ref[...] + mod_ref[:, 5 * D:6 * D] * y


def _combine(pos, hmid, xn, mod, wt, wsd, ys, off, layer):
    nb, t, _ = xn.shape
    nt = t // TM
    row = lambda b, i, *_: (b, i, 0)
    const = lambda b, i, *_: (layer, 0, 0)
    return pl.pallas_call(
        functools.partial(_combine_kernel, nt=nt),
        grid_spec=pltpu.PrefetchScalarGridSpec(
            num_scalar_prefetch=1,
            grid=(nb, nt),
            in_specs=[
                pl.BlockSpec((None, TM, D_EXP), row),
                pl.BlockSpec((None, TM, D), row),
                pl.BlockSpec((None, 1, ADA * D), lambda b, i, *_: (jnp.where(i + off == 0, nb, b), 0, 0)),
                pl.BlockSpec((None, TM, 8), row),
                _resident((None, D_EXP, D), const),
                pl.BlockSpec(memory_space=pl.ANY),
            ],
            out_specs=pl.BlockSpec((None, TM, D), row),
            scratch_shapes=[pltpu.VMEM((2, TM, D), F32), pltpu.SemaphoreType.DMA],
        ),
        out_shape=jax.ShapeDtypeStruct((nb, t, D), F32),
        compiler_params=_cparams(("arbitrary", "arbitrary")),
        name="moe_combine",
    )(pos, hmid, xn, mod, wt, wsd, ys)


def _moe_plan(route):
    nb, _, tp = route.shape
    n_tok = nb * tp
    idx = route[:, 0:2, :].astype(I32).transpose(0, 2, 1).reshape(n_tok * 2)
    wt = jnp.pad(route[:, 2:4, :].transpose(0, 2, 1), ((0, 0), (0, 0), (0, 6)))
    onehot = (idx[:, None] == jnp.arange(N_EXP, dtype=I32)[None, :]).astype(I32)
    csum = jnp.cumsum(onehot, axis=0)
    counts = csum[-1]
    rank = jnp.sum((csum - onehot) * onehot, axis=1)
    padded = ((counts + TE - 1) // TE) * TE
    ends = jnp.cumsum(padded)
    starts = ends - padded
    pos = (jnp.sum(onehot * starts[None, :], axis=1) + rank).astype(I32)
    n_tiles = (2 * n_tok) // TE + N_EXP
    n_used = (ends[-1] // TE).astype(I32)
    tile_block = jnp.minimum(jnp.arange(n_tiles, dtype=I32), n_used - 1)
    tile_expert = jnp.sum((tile_block[:, None] * TE >= ends[None, :]).astype(I32), axis=1).astype(I32)
    npads = padded - counts
    pad_ends = jnp.cumsum(npads)
    pad_starts = pad_ends - npads
    qidx = jnp.arange(N_EXP * TE, dtype=I32)
    pe = jnp.minimum(jnp.sum((qidx[:, None] >= pad_ends[None, :]).astype(I32), axis=1), N_EXP - 1)
    in_seg = starts[pe] + counts[pe] + (qidx - pad_starts[pe])
    tail = ends[-1] + (qidx - pad_ends[-1])
    padpos = jnp.where(qidx < pad_ends[-1], in_seg, tail).astype(I32)
    return pos, padpos, tile_expert, tile_block, n_used.reshape(1), wt, n_tiles * TE


def _moe(h2, xn, route, mod, w, off, layer):
    pos, padpos, tile_expert, tile_block, n_used, wt, n_rows = _moe_plan(route)
    xs, hmid = _dispatch(h2, pos, padpos, n_rows, w["wsg"], w["wsu"], layer)
    ys = _experts(xs, tile_expert, tile_block, n_used, w["wg"], w["wu"], w["wd"], layer)
    return _combine(pos, hmid.reshape(xn.shape[0], xn.shape[1], D_EXP), xn, mod, wt, w["wsd"], ys, off, layer)


def _rope_tables(t, s):
    half = MLA_ROPE // 2
    inv_freq = ROPE_THETA ** (-jnp.arange(0, half, 2, dtype=F32) / half)
    tt = jnp.arange(t - s)
    ang_row = (tt // GRID_W).astype(F32)[:, None] * inv_freq
    ang_col = (tt % GRID_W).astype(F32)[:, None] * inv_freq
    ang = jnp.concatenate([ang_row] * 2 + [ang_col] * 2, axis=1)
    cos = jnp.cos(ang)
    sin = jnp.sin(ang)
    sign = np.where((np.arange(MLA_ROPE) % 32) < 16, -1.0, 1.0).astype(np.float32)
    sin = sin * sign[None, :]
    cos = jnp.concatenate([jnp.ones((s, MLA_ROPE), F32), cos], axis=0)
    sin = jnp.concatenate([jnp.zeros((s, MLA_ROPE), F32), sin], axis=0)
    cos = jnp.pad(cos, ((0, 0), (0, LANE - MLA_ROPE)), constant_values=1.0)
    sin = jnp.pad(sin, ((0, 0), (0, LANE - MLA_ROPE)))
    return cos.astype(F32), sin.astype(F32)


def _ab_weights(j, t, s, ab_w_in, lru_conv_w, lru_conv_b, lru_w_a, lru_b_a, lru_w_i, lru_b_i, lru_lambda,
                mla_q_norm_g, mla_w_uq, mla_kv_norm_g, mla_w_ukv, mla_q_head_g, mla_k_head_g):
    w_in = jnp.pad(ab_w_in[j], ((0, 0), (0, AB_IN_PAD - AB_IN))).astype(BF16)
    uq = mla_w_uq[j].reshape(Q_RANK, MLA_H, MLA_QK)
    uq_n = uq[:, :, :MLA_NOPE].reshape(Q_RANK, MLA_H * LANE)
    uq_r = jnp.pad(uq[:, :, MLA_NOPE:], ((0, 0), (0, 0), (0, LANE - MLA_ROPE))).reshape(Q_RANK, MLA_H * LANE)
    ukv = mla_w_ukv[j].reshape(KV_RANK, MLA_H, MLA_NOPE + MLA_V)
    ukv_k = ukv[:, :, :MLA_NOPE].reshape(KV_RANK, MLA_H * LANE)
    ukv_v = ukv[:, :, MLA_NOPE:].reshape(KV_RANK, MLA_H * MLA_V)
    scale = MLA_QK ** -0.5 * LOG2E
    qg = mla_q_head_g[j] * scale
    kg = mla_k_head_g[j]
    pad_r = lambda g: jnp.pad(g[MLA_NOPE:], (0, LANE - MLA_ROPE))
    hg = jnp.stack([qg[:MLA_NOPE], pad_r(qg), kg[:MLA_NOPE], pad_r(kg)], axis=0)
    cos, sin = _rope_tables(t, s)
    return dict(
        w_in=w_in, qng=mla_q_norm_g[j][None], kvng=mla_kv_norm_g[j][None],
        w_uq=jnp.concatenate([uq_n, uq_r], axis=1).astype(BF16),
        w_ukv=jnp.concatenate([ukv_k, ukv_v], axis=1).astype(BF16),
        hg=hg, cos=cos, sin=sin,
        conv_w=lru_conv_w[j], conv_b=lru_conv_b[j][None],
        w_a=lru_w_a[j].astype(BF16), b_a=lru_b_a[j][:, None, :],
        w_i=lru_w_i[j].astype(BF16), b_i=lru_b_i[j][:, None, :], lam=lru_lambda[j][:, None, :],
    )


def kernel(x, c, ctx, c_ctx, ada_w, ada_b, norm_mix_g, norm_ffn_g, ab_w_in, ab_w_out, lru_conv_w, lru_conv_b, lru_w_a, lru_b_a, lru_w_i, lru_b_i, lru_lambda, mla_q_norm_g, mla_w_uq, mla_kv_norm_g, mla_w_ukv, mla_q_head_g, mla_k_head_g, na_w_qkv, na_w_out, na_q_head_g, na_k_head_g, na_rpb, router_w, router_b, moe_w_gate, moe_w_up, moe_w_down, moe_ws_gate, moe_ws_up, moe_ws_down):
    nb, seq, _ = x.shape
    s = ctx.shape[1]
    depth = ada_w.shape[0]
    assert s == TM and seq % TM == 0 and nb < 8
    t = s + seq
    rows = seq // GRID_W

    cs = jnp.zeros((8, D), F32).at[:nb].set(c).at[nb].set(c_ctx)
    mod_all = _adaln(cs, ada_w, ada_b).reshape(depth, 8, 1, ADA * D)

    rw_t = router_w.T
    rw_hi = rw_t.astype(BF16)
    rw_lo = jnp.concatenate([rw_hi, (rw_t - rw_hi.astype(F32)).astype(BF16)], axis=0)
    rb = router_b.reshape(N_EXP, 1)

    moe_w = dict(wg=moe_w_gate.astype(BF16), wu=moe_w_up.astype(BF16), wd=moe_w_down.astype(BF16),
                 wsg=moe_ws_gate.astype(BF16), wsu=moe_ws_up.astype(BF16), wsd=moe_ws_down.astype(BF16))

    x_all = (ctx, x)
    for i in range(depth):
        last = i == depth - 1
        off = 1 if last else 0
        j = i // 2
        mod = mod_all[i]
        g_mix = norm_mix_g[i][None]
        g_ffn = norm_ffn_g[i][None]
        if i % 2 == 0:
            w = _ab_weights(j, t, s, ab_w_in, lru_conv_w, lru_conv_b, lru_w_a, lru_b_a, lru_w_i, lru_b_i, lru_lambda,
                            mla_q_norm_g, mla_w_uq, mla_kv_norm_g, mla_w_ukv, mla_q_head_g, mla_k_head_g)
            ux, gate, q, k, v = _ab_in(x_all, nb, t, g_mix, mod, w)
            h_f, h_b = _lru(ux, w)
            att_c, att_l = _mla_attn(q, k, v)
            xn, h2, route = _mixer_out(functools.partial(_ab_out_kernel, off=off), (h_f, h_b, gate, att_c, att_l),
                                       x_all, nb, t, mod, g_ffn, ab_w_out[j].astype(BF16), rw_hi, rw_lo, rb, off)
        else:
            scale = NA_DH ** -0.5 * LOG2E
            hg = jnp.stack([na_q_head_g[j] * scale, na_k_head_g[j], jnp.ones((NA_DH,), F32)], axis=0)[:, None, :]
            qkv = _na_in(x_all, g_mix, mod, na_w_qkv[j].astype(BF16), hg)
            bias, variant = _na_bias_tables(na_rpb[j] * LOG2E, rows)
            att = _na_attn(qkv, bias, variant)
            assert last, "the neighbourhood layer has no context-output path"
            xn, h2, route = _mixer_out(_na_out_kernel, (att,), x_all, nb, t, mod, g_ffn, na_w_out[j].astype(BF16),
                                       rw_hi, rw_lo, rb, off)
        x_all = _moe(h2, xn, route, mod, moe_w, off, i)
    return x_all
```

```python
import functools

import numpy as np
import jax
import jax.numpy as jnp
from jax import lax
from jax.experimental import pallas as pl
from jax.experimental.pallas import tpu as pltpu

F32 = jnp.float32
BF16 = jnp.bfloat16
I32 = jnp.int32

D = 2048
ADA = 6
EPS = 1e-6
NEG = -1e30
LOG2E = 1.4426950408889634
GRID_W = 64
D_RNN = 1024
LRU_BLOCKS = 8
LRU_BW = 128
LRU_C = 8.0
MLA_H = 8
MLA_NOPE = 128
MLA_ROPE = 64
MLA_QK = 192
MLA_V = 128
Q_RANK = 512
KV_RANK = 512
ROPE_THETA = 10000.0
AB_IN = 3136
AB_IN_PAD = 3200
NA_H = 16
NA_DH = 128
NA_ROWS = 8
NA_COLS = 16
N_EXP = 16
N_GROUPS = 4
EXP_PER_GROUP = 4
D_EXP = 1408

LANE = 128
SUBLANES = 8
TM = 256
TE = 256
VMEM_LIMIT = 56 * 1024 * 1024

NA_QR = 4
NA_KR = 12


def _cparams(sem, vmem=VMEM_LIMIT):
    return pltpu.CompilerParams(dimension_semantics=sem, vmem_limit_bytes=vmem)


def _resident(shape, index_map):
    return pl.BlockSpec(shape, index_map, pipeline_mode=pl.Buffered(1))


def _rms(x, g, n):
    ms = jnp.sum(x * x, axis=-1, keepdims=True) * (1.0 / n)
    return (x * lax.rsqrt(ms + EPS)) * g


def _modulate(x, g, shift, scale):
    return _rms(x, g, D) * (1.0 + scale) + shift


def _gelu_tanh(x):
    c = 0.7978845608028654
    return x * (0.5 * (1.0 + jnp.tanh(c * (x + 0.044715 * (x * x * x)))))


def _silu(x):
    return x * jax.nn.sigmoid(x)


ADA_TN = 1536


def _adaln_kernel(cs_ref, w_ref, b_ref, o_ref):
    a = _silu(cs_ref[...]).astype(BF16)
    o_ref[...] = jnp.dot(a, w_ref[...].astype(BF16), preferred_element_type=F32) + b_ref[...]


def _adaln(cs, ada_w, ada_b):
    depth = ada_w.shape[0]
    n = ADA * D
    return pl.pallas_call(
        _adaln_kernel,
        grid=(depth, n // ADA_TN),
        in_specs=[
            pl.BlockSpec((8, D), lambda l, j: (0, 0)),
            pl.BlockSpec((None, D, ADA_TN), lambda l, j: (l, 0, j)),
            pl.BlockSpec((None, 1, ADA_TN), lambda l, j: (l, 0, j)),
        ],
        out_specs=pl.BlockSpec((None, 8, ADA_TN), lambda l, j: (l, 0, j)),
        out_shape=jax.ShapeDtypeStruct((depth, 8, n), F32),
        compiler_params=_cparams(("arbitrary", "arbitrary")),
        name="adaln",
    )(cs, ada_w, ada_b.reshape(depth, 1, n))


def _mod_spec(nb, off):
    return pl.BlockSpec((None, 1, ADA * D), lambda b, i: (jnp.where(i + off == 0, nb, b), 0, 0))


def _split_x(x_in, off):
    ctx_map = lambda b, i: (b, 0, 0)
    if isinstance(x_in, tuple):
        return x_in, ctx_map, lambda b, i: (b, jnp.maximum(i + off - 1, 0), 0)
    return (x_in, x_in), ctx_map, lambda b, i: (b, jnp.maximum(i + off, 1), 0)


def _ab_in_kernel(xc_ref, xl_ref, g_ref, mod_ref, win_ref, qng_ref, kvng_ref, wuq_ref, wukv_ref, hg_ref, cos_ref,
                  sin_ref, ux_ref, gate_ref, q_ref, k_ref, v_ref):
    x = jnp.where(pl.program_id(1) == 0, xc_ref[...], xl_ref[...])
    h = _modulate(x, g_ref[...], mod_ref[:, 0:D], mod_ref[:, D:2 * D]).astype(BF16)
    u = jnp.dot(h, win_ref[...], preferred_element_type=F32)
    ux_ref[...] = u[:, 0:D_RNN].astype(BF16)
    gate_ref[...] = _gelu_tanh(u[:, D_RNN:2 * D_RNN]).astype(BF16)
    o2 = 2 * D_RNN
    cq = _rms(u[:, o2:o2 + Q_RANK], qng_ref[...], Q_RANK).astype(BF16)
    ckv = _rms(u[:, o2 + Q_RANK:o2 + Q_RANK + KV_RANK], kvng_ref[...], KV_RANK).astype(BF16)
    qf = jnp.dot(cq, wuq_ref[...], preferred_element_type=F32)
    kvf = jnp.dot(ckv, wukv_ref[...], preferred_element_type=F32)
    kr = u[:, o2 + Q_RANK + KV_RANK:AB_IN_PAD]
    cos = cos_ref[...]
    sin = sin_ref[...]
    lane = lax.broadcasted_iota(I32, (TM, LANE), 1)
    first_half = (lane % 32) < 16

    def rope(z):
        partner = jnp.where(first_half, pltpu.roll(z, LANE - 16, 1), pltpu.roll(z, 16, 1))
        return z * cos + partner * sin

    hg = hg_ref[...]
    kr_ss = jnp.sum(kr * kr, axis=-1, keepdims=True)
    nh = MLA_H * MLA_NOPE
    for hh in range(MLA_H):
        qn = qf[:, hh * LANE:(hh + 1) * LANE]
        qr = qf[:, nh + hh * LANE:nh + (hh + 1) * LANE]
        ss = jnp.sum(qn * qn, axis=-1, keepdims=True) + jnp.sum(qr * qr, axis=-1, keepdims=True)
        inv = lax.rsqrt(ss * (1.0 / MLA_QK) + EPS)
        q_ref[:, hh * 256:hh * 256 + LANE] = ((qn * inv) * hg[0:1]).astype(BF16)
        q_ref[:, hh * 256 + LANE:(hh + 1) * 256] = rope((qr * inv) * hg[1:2]).astype(BF16)
        kn = kvf[:, hh * LANE:(hh + 1) * LANE]
        ss = jnp.sum(kn * kn, axis=-1, keepdims=True) + kr_ss
        inv = lax.rsqrt(ss * (1.0 / MLA_QK) + EPS)
        k_ref[:, hh * 256:hh * 256 + LANE] = ((kn * inv) * hg[2:3]).astype(BF16)
        k_ref[:, hh * 256 + LANE:(hh + 1) * 256] = rope((kr * inv) * hg[3:4]).astype(BF16)
    v_ref[...] = kvf[:, nh:].astype(BF16)


def _ab_in(x_in, nb, t, g, mod, w):
    nt = t // TM
    (x_c, x_l), ctx_map, lat_map = _split_x(x_in, 0)
    row = lambda b, i: (b, i, 0)
    const = lambda b, i: (0, 0)
    outs = [
        jax.ShapeDtypeStruct((nb, t, D_RNN), BF16),
        jax.ShapeDtypeStruct((nb, t, D_RNN), BF16),
        jax.ShapeDtypeStruct((nb, t, MLA_H * 256), BF16),
        jax.ShapeDtypeStruct((nb, t, MLA_H * 256), BF16),
        jax.ShapeDtypeStruct((nb, t, MLA_H * MLA_V), BF16),
    ]
    return pl.pallas_call(
        _ab_in_kernel,
        grid=(nb, nt),
        in_specs=[
            pl.BlockSpec((None, TM, D), ctx_map),
            pl.BlockSpec((None, TM, D), lat_map),
            _resident((1, D), const),
            _mod_spec(nb, 0),
            _resident((D, AB_IN_PAD), const),
            _resident((1, Q_RANK), const),
            _resident((1, KV_RANK), const),
            _resident((Q_RANK, 2 * MLA_H * LANE), const),
            _resident((KV_RANK, 2 * MLA_H * LANE), const),
            _resident((4, LANE), const),
            pl.BlockSpec((TM, LANE), lambda b, i: (i, 0)),
            pl.BlockSpec((TM, LANE), lambda b, i: (i, 0)),
        ],
        out_specs=[
            pl.BlockSpec((None, TM, D_RNN), row),
            pl.BlockSpec((None, TM, D_RNN), row),
            pl.BlockSpec((None, TM, MLA_H * 256), row),
            pl.BlockSpec((None, TM, MLA_H * 256), row),
            pl.BlockSpec((None, TM, MLA_H * MLA_V), row),
        ],
        out_shape=outs,
        compiler_params=_cparams(("arbitrary", "arbitrary")),
        name="ab_in",
    )(x_c, x_l, g, mod, w["w_in"], w["qng"], w["kvng"], w["w_uq"], w["w_ukv"], w["hg"], w["cos"], w["sin"])


def _lru_dir(prev_ref, cur_ref, next_ref, c, n_chunks, d, reverse,
             cw_ref, cb_ref, wa_ref, ba_ref, wi_ref, bi_ref, lam_ref, carry_ref, out_ref):
    xc = cur_ref[...].astype(F32)
    row = lax.broadcasted_iota(I32, (TM, 1), 0)
    left_ok = jnp.where(c >= 2, 1.0, 0.0)
    right_ok = jnp.where(jnp.logical_and(c >= 1, c <= n_chunks - 2), 1.0, 0.0)
    p0 = prev_ref[TM - 2:TM - 1, :].astype(F32) * left_ok
    p1 = prev_ref[TM - 1:TM, :].astype(F32) * left_ok
    n0 = next_ref[0:1, :].astype(F32) * right_ok
    x_m2 = jnp.where(row >= 2, pltpu.roll(xc, 2, 0), jnp.where(row == 0, p0, p1))
    x_m1 = jnp.where(row >= 1, pltpu.roll(xc, 1, 0), p1)
    x_p1 = jnp.where(row <= TM - 2, pltpu.roll(xc, TM - 1, 0), n0)
    cw = cw_ref[...]
    xconv = x_m2 * cw[0:1] + x_m1 * cw[1:2] + xc * cw[2:3] + x_p1 * cw[3:4] + cb_ref[...]

    xb = xconv.astype(BF16)
    ra = []
    ri = []
    for n in range(LRU_BLOCKS):
        xs = xb[:, n * LRU_BW:(n + 1) * LRU_BW]
        ra.append(jnp.dot(xs, wa_ref[d, n], preferred_element_type=F32))
        ri.append(jnp.dot(xs, wi_ref[d, n], preferred_element_type=F32))
    r = jax.nn.sigmoid(jnp.concatenate(ra, axis=1) + ba_ref[d])
    gi = jax.nn.sigmoid(jnp.concatenate(ri, axis=1) + bi_ref[d])
    neg_lam = -lam_ref[d]
    softplus = jnp.maximum(neg_lam, 0.0) + jnp.log1p(jnp.exp(-jnp.abs(neg_lam)))
    log_a = (-LRU_C * r) * softplus
    a = jnp.exp(log_a)
    th = jnp.tanh(log_a)
    neg_expm1 = (-2.0 * th) / (1.0 - th)
    bb = jnp.sqrt(neg_expm1) * gi * xconv

    ng = TM // SUBLANES
    a3 = a.reshape(ng, SUBLANES, D_RNN)
    b3 = bb.reshape(ng, SUBLANES, D_RNN)
    sub = lax.broadcasted_iota(I32, (1, SUBLANES, 1), 1)
    k = 1
    while k < SUBLANES:
        if reverse:
            keep = sub < SUBLANES - k
            a_s = jnp.where(keep, pltpu.roll(a3, SUBLANES - k, 1), 1.0)
            b_s = jnp.where(keep, pltpu.roll(b3, SUBLANES - k, 1), 0.0)
        else:
            keep = sub >= k
            a_s = jnp.where(keep, pltpu.roll(a3, k, 1), 1.0)
            b_s = jnp.where(keep, pltpu.roll(b3, k, 1), 0.0)
        b3 = b3 + a3 * b_s
        a3 = a3 * a_s
        k *= 2
    hcar = carry_ref[d:d + 1, :]
    last = 0 if reverse else SUBLANES - 1
    for g in (range(ng - 1, -1, -1) if reverse else range(ng)):
        h = b3[g] + a3[g] * hcar
        out_ref[g * SUBLANES:(g + 1) * SUBLANES, :] = h
        hcar = h[last:last + 1, :]
    carry_ref[d:d + 1, :] = hcar


def _lru_kernel(pf_ref, cf_ref, nf_ref, pb_ref, cbk_ref, nb_ref, cw_ref, cb_ref, wa_ref, ba_ref, wi_ref, bi_ref, lam_ref,
                hf_ref, hb_ref, carry_ref, *, n_chunks):
    s = pl.program_id(1)

    @pl.when(s == 0)
    def _():
        carry_ref[...] = jnp.zeros_like(carry_ref)

    params = (cw_ref, cb_ref, wa_ref, ba_ref, wi_ref, bi_ref, lam_ref, carry_ref)
    _lru_dir(pf_ref, cf_ref, nf_ref, s, n_chunks, 0, False, *params, hf_ref)
    cb_idx = jnp.where(s == 0, 0, n_chunks - s)
    _lru_dir(pb_ref, cbk_ref, nb_ref, cb_idx, n_chunks, 1, True, *params, hb_ref)


def _lru(ux, w):
    nb, t, _ = ux.shape
    nt = t // TM
    fwd = lambda s: s
    bwd = lambda s: jnp.where(s == 0, 0, nt - s)
    prev = lambda c: jnp.maximum(c - 1, 0)
    nxt = lambda c: jnp.minimum(c + 1, nt - 1)
    blk = (None, TM, D_RNN)
    const2 = lambda b, s: (0, 0)
    const3 = lambda b, s: (0, 0, 0)
    const4 = lambda b, s: (0, 0, 0, 0)
    return pl.pallas_call(
        functools.partial(_lru_kernel, n_chunks=nt),
        grid=(nb, nt),
        in_specs=[
            pl.BlockSpec(blk, lambda b, s: (b, prev(fwd(s)), 0)),
            pl.BlockSpec(blk, lambda b, s: (b, fwd(s), 0)),
            pl.BlockSpec(blk, lambda b, s: (b, nxt(fwd(s)), 0)),
            pl.BlockSpec(blk, lambda b, s: (b, prev(bwd(s)), 0)),
            pl.BlockSpec(blk, lambda b, s: (b, bwd(s), 0)),
            pl.BlockSpec(blk, lambda b, s: (b, nxt(bwd(s)), 0)),
            _resident((4, D_RNN), const2),
            _resident((1, D_RNN), const2),
            _resident((2, LRU_BLOCKS, LRU_BW, LRU_BW), const4),
            _resident((2, 1, D_RNN), const3),
            _resident((2, LRU_BLOCKS, LRU_BW, LRU_BW), const4),
            _resident((2, 1, D_RNN), const3),
            _resident((2, 1, D_RNN), const3),
        ],
        out_specs=[
            pl.BlockSpec(blk, lambda b, s: (b, fwd(s), 0)),
            pl.BlockSpec(blk, lambda b, s: (b, bwd(s), 0)),
        ],
        out_shape=[jax.ShapeDtypeStruct((nb, t, D_RNN), F32)] * 2,
        scratch_shapes=[pltpu.VMEM((8, D_RNN), F32)],
        compiler_params=_cparams(("arbitrary", "arbitrary")),
        name="lru_scan",
    )(ux, ux, ux, ux, ux, ux, w["conv_w"], w["conv_b"], w["w_a"], w["b_a"], w["w_i"], w["b_i"], w["lam"])


def _softmax_av(q, k, v):
    s = lax.dot_general(q, k, (((1,), (1,)), ((), ())), preferred_element_type=F32)
    m = jnp.max(s, axis=-1, keepdims=True)
    p = jnp.exp2(s - m)
    l = jnp.sum(p, axis=-1, keepdims=True)
    o = jnp.dot(p.astype(BF16), v, preferred_element_type=F32)
    return o / l


MLA_HPS = 4


MLA_TQ = 256


def _mla_attn_kernel(q_ref, k_ref, v_ref, o_ref, *, hps):
    q2d = q_ref.at[0] if len(q_ref.shape) == 3 else q_ref
    for hh in range(hps):
        qs = slice(hh * 256, (hh + 1) * 256)
        vs = slice(hh * MLA_V, (hh + 1) * MLA_V)
        o_ref[:, vs] = _softmax_av(q2d[:, qs], k_ref[:, qs], v_ref[:, vs]).astype(BF16)


def _mla_attn(q, k, v):
    nb, t, _ = q.shape
    lat = t - TM
    hps_c = MLA_H
    ctx_out = pl.pallas_call(
        functools.partial(_mla_attn_kernel, hps=hps_c),
        grid=(nb,),
        in_specs=[
            pl.BlockSpec((None, TM, hps_c * 256), lambda b: (b, 0, 0)),
            pl.BlockSpec((None, TM, hps_c * 256), lambda b: (b, 0, 0)),
            pl.BlockSpec((None, TM, hps_c * MLA_V), lambda b: (b, 0, 0)),
        ],
        out_specs=pl.BlockSpec((None, TM, hps_c * MLA_V), lambda b: (b, 0, 0)),
        out_shape=jax.ShapeDtypeStruct((nb, TM, MLA_H * MLA_V), BF16),
        compiler_params=_cparams(("arbitrary",)),
        name="mla_attn_ctx",
    )(q, k, v)
    hps = MLA_HPS
    lat_out = pl.pallas_call(
        functools.partial(_mla_attn_kernel, hps=hps),
        grid=(nb, MLA_H // hps, lat // MLA_TQ),
        in_specs=[
            pl.BlockSpec((pl.Element(MLA_TQ), pl.Element(hps * 256)),
                         lambda b, h, i: (pl.multiple_of(b * t + TM + i * MLA_TQ, TM),
                                          pl.multiple_of(h * (hps * 256), 256))),
            pl.BlockSpec((None, t, hps * 256), lambda b, h, i: (b, 0, h)),
            pl.BlockSpec((None, t, hps * MLA_V), lambda b, h, i: (b, 0, h)),
        ],
        out_specs=pl.BlockSpec((None, MLA_TQ, hps * MLA_V), lambda b, h, i: (b, i, h)),
        out_shape=jax.ShapeDtypeStruct((nb, lat, MLA_H * MLA_V), BF16),
        compiler_params=_cparams(("arbitrary", "arbitrary", "arbitrary")),
        name="mla_attn",
    )(q.reshape(nb * t, MLA_H * 256), k, v)
    return ctx_out, lat_out


def _route(h2, rwh_ref, rwl_ref, rb_ref, route_ref):
    hi = h2.astype(BF16)
    lo = (h2 - hi.astype(F32)).astype(BF16)
    nt_dims = (((1,), (1,)), ((), ()))
    both = lax.dot_general(rwl_ref[...], hi, nt_dims, preferred_element_type=F32)
    logits = (both[0:N_EXP] + lax.dot_general(rwh_ref[...], lo, nt_dims, preferred_element_type=F32)
              + both[N_EXP:])
    scores = jax.nn.sigmoid(logits)
    sel = scores + rb_ref[...]
    sc = [scores[e:e + 1, :] for e in range(N_EXP)]
    se = [sel[e:e + 1, :] for e in range(N_EXP)]
    gs = []
    for g in range(N_GROUPS):
        a, b, c, d = se[4 * g:4 * g + 4]
        hi_ab, lo_ab = jnp.maximum(a, b), jnp.minimum(a, b)
        hi_cd, lo_cd = jnp.maximum(c, d), jnp.minimum(c, d)
        top1 = jnp.maximum(hi_ab, hi_cd)
        top2 = jnp.maximum(jnp.maximum(lo_ab, lo_cd), jnp.minimum(hi_ab, hi_cd))
        gs.append(top1 + top2)
    best = jnp.zeros_like(gs[0])
    best_v = gs[0]
    for g in range(1, N_GROUPS):
        upd = gs[g] > best_v
        best = jnp.where(upd, float(g), best)
        best_v = jnp.where(upd, gs[g], best_v)
    masked = [jnp.where(best == float(e // EXP_PER_GROUP), se[e], NEG) for e in range(N_EXP)]
    i1 = jnp.zeros_like(best)
    v1 = masked[0]
    s1 = sc[0]
    for e in range(1, N_EXP):
        upd = masked[e] > v1
        i1 = jnp.where(upd, float(e), i1)
        v1 = jnp.where(upd, masked[e], v1)
        s1 = jnp.where(upd, sc[e], s1)
    i2 = jnp.zeros_like(best)
    v2 = jnp.full_like(v1, -jnp.inf)
    s2 = jnp.zeros_like(s1)
    for e in range(N_EXP):
        upd = jnp.logical_and(masked[e] > v2, i1 != float(e))
        i2 = jnp.where(upd, float(e), i2)
        v2 = jnp.where(upd, masked[e], v2)
        s2 = jnp.where(upd, sc[e], s2)
    tot = s1 + s2
    route_ref[0:1, :] = i1
    route_ref[1:2, :] = i2
    route_ref[2:3, :] = s1 / tot
    route_ref[3:4, :] = s2 / tot
    route_ref[4:8, :] = jnp.zeros((4, TM), F32)


def _finish_mixer(y, x, mod_ref, g_ref, rwh_ref, rwl_ref, rb_ref, xn_ref, h2_ref, route_ref):
    xn = x + mod_ref[:, 2 * D:3 * D] * y
    xn_ref[...] = xn
    h2 = _modulate(xn, g_ref[...], mod_ref[:, 3 * D:4 * D], mod_ref[:, 4 * D:5 * D])
    h2_ref[...] = h2
    _route(h2, rwh_ref, rwl_ref, rb_ref, route_ref)


def _ab_out_kernel(hf_ref, hb_ref, gate_ref, attc_ref, attl_ref, xc_ref, xl_ref, mod_ref, g_ref, wo_ref, rwh_ref,
                   rwl_ref, rb_ref, xn_ref, h2_ref, route_ref, *, off):
    rnn = ((hf_ref[...] + hb_ref[...]) * gate_ref[...].astype(F32)).astype(BF16)
    is_ctx = pl.program_id(1) + off == 0
    att = jnp.where(is_ctx, attc_ref[...], attl_ref[...])
    y = (jnp.dot(rnn, wo_ref[0:D_RNN, :], preferred_element_type=F32)
         + jnp.dot(att, wo_ref[D_RNN:, :], preferred_element_type=F32))
    x = jnp.where(is_ctx, xc_ref[...], xl_ref[...])
    _finish_mixer(y, x, mod_ref, g_ref, rwh_ref, rwl_ref, rb_ref, xn_ref, h2_ref, route_ref)


def _na_out_kernel(att_ref, x_ref, mod_ref, g_ref, wo_ref, rwh_ref, rwl_ref, rb_ref, xn_ref, h2_ref, route_ref):
    x2d = x_ref.at[0] if len(x_ref.shape) == 3 else x_ref
    for sub in range(att_ref.shape[0] // TM):
        r = pl.ds(sub * TM, TM)
        y = jnp.dot(att_ref[r, :], wo_ref[...], preferred_element_type=F32)
        _finish_mixer(y, x2d[r, :], mod_ref, g_ref, rwh_ref, rwl_ref, rb_ref,
                      xn_ref.at[r, :], h2_ref.at[r, :], route_ref.at[:, r])


def _mixer_out(kernel, acts, x_in, nb, t, mod, g, wo, rw_hi, rw_lo, rb, off):
    nt = t // TM - off
    sub = 2 if (off == 1 and nt % 2 == 0 and kernel is _na_out_kernel) else 1
    rows = sub * TM
    out_row = lambda b, i: (b, i, 0)
    const = lambda b, i: (0, 0)
    row = lambda b, i: (b, i + off, 0)
    if kernel is _na_out_kernel:
        x_arrays = (x_in,)
        if sub == 1:
            x_specs = [pl.BlockSpec((None, TM, D), row)]
        else:
            x_specs = [pl.BlockSpec((pl.Element(1), pl.Element(rows), pl.Element(D)),
                                    lambda b, i: (b, (sub * i + off) * TM, 0))]
    else:
        x_arrays, ctx_map, lat_map = _split_x(x_in, off)
        x_specs = [pl.BlockSpec((None, TM, D), ctx_map), pl.BlockSpec((None, TM, D), lat_map)]
    def act_map(a):
        if a.shape[1] == t:
            return row
        if a.shape[1] == TM:
            return lambda b, i: (b, 0, 0)
        if off == 0:
            return lambda b, i: (b, jnp.maximum(i - 1, 0), 0)
        return out_row

    act_specs = [pl.BlockSpec((None, rows, a.shape[2]), act_map(a)) for a in acts]
    assert sub == 1 or all(a.shape[1] == t - TM for a in acts)
    return pl.pallas_call(
        kernel,
        grid=(nb, nt // sub),
        in_specs=act_specs + x_specs + [
            _mod_spec(nb, off),
            _resident((1, D), const),
            _resident(wo.shape, const),
            _resident((N_EXP, D), const),
            _resident((2 * N_EXP, D), const),
            _resident((N_EXP, 1), const),
        ],
        out_specs=[
            pl.BlockSpec((None, rows, D), out_row),
            pl.BlockSpec((None, rows, D), out_row),
            pl.BlockSpec((None, 8, rows), lambda b, i: (b, 0, i)),
        ],
        out_shape=[
            jax.ShapeDtypeStruct((nb, nt * TM, D), F32),
            jax.ShapeDtypeStruct((nb, nt * TM, D), F32),
            jax.ShapeDtypeStruct((nb, 8, nt * TM), F32),
        ],
        compiler_params=_cparams(("arbitrary", "arbitrary")),
        name="mixer_out",
    )(*acts, *x_arrays, mod, g, wo, rw_hi, rw_lo, rb)


def _na_in_kernel(x_ref, g_ref, mod_ref, w_ref, hg_ref, o_ref):
    h = _modulate(x_ref[...], g_ref[...], mod_ref[:, 0:D], mod_ref[:, D:2 * D]).astype(BF16)
    for p in range(3):
        u = jnp.dot(h, w_ref[:, p * D:(p + 1) * D], preferred_element_type=F32)
        if p == 2:
            o_ref[p] = u.astype(BF16)
            continue
        hg = hg_ref[p]
        for hh in range(NA_H):
            uh = u[:, hh * NA_DH:(hh + 1) * NA_DH]
            o_ref[p, :, hh * NA_DH:(hh + 1) * NA_DH] = _rms(uh, hg, NA_DH).astype(BF16)


def _na_in(x_all, g, mod, w_qkv, hg):
    nb, t, _ = x_all.shape
    nt = t // TM
    return pl.pallas_call(
        _na_in_kernel,
        grid=(nb, nt),
        in_specs=[
            pl.BlockSpec((None, TM, D), lambda b, i: (b, i, 0)),
            _resident((1, D), lambda b, i: (0, 0)),
            _mod_spec(nb, 0),
            _resident((D, 3 * D), lambda b, i: (0, 0)),
            _resident((3, 1, NA_DH), lambda b, i: (0, 0, 0)),
        ],
        out_specs=pl.BlockSpec((3, None, TM, D), lambda b, i: (0, b, i, 0)),
        out_shape=jax.ShapeDtypeStruct((3, nb, t, D), BF16),
        compiler_params=_cparams(("arbitrary", "arbitrary")),
        name="na_in",
    )(x_all, g, mod, w_qkv, hg)


def _na_key_base(rb, rows):
    return jnp.clip(rb * NA_QR - NA_ROWS // 2, 0, rows - NA_KR)


NA_HPS = 4


NA_RPS = 2


def _na_attn_kernel(var_ref, q_ref, k_ref, v_ref, bias_ref, o_ref, *, rows):
    nk = NA_KR * GRID_W
    nt_dims = (((1,), (1,)), ((), ()))
    for sb in range(NA_RPS):
        rb = pl.program_id(2) * NA_RPS + sb
        start = pl.multiple_of(TM + _na_key_base(rb, rows) * GRID_W, GRID_W)
        var = var_ref[rb]
        qrows = slice(sb * TM, (sb + 1) * TM)
        for hh in range(NA_HPS):
            hs = slice(hh * NA_DH, (hh + 1) * NA_DH)
            q = q_ref[qrows, hs]
            s_c = lax.dot_general(q, k_ref[0:TM, hs], nt_dims, preferred_element_type=F32)
            s_l = (lax.dot_general(q, k_ref[pl.ds(start, nk), hs], nt_dims, preferred_element_type=F32)
                   + bias_ref[hh, var])
            m = jnp.maximum(jnp.max(s_c, axis=-1, keepdims=True), jnp.max(s_l, axis=-1, keepdims=True))
            p_c = jnp.exp2(s_c - m)
            p_l = jnp.exp2(s_l - m)
            l = jnp.sum(p_c, axis=-1, keepdims=True) + jnp.sum(p_l, axis=-1, keepdims=True)
            o = (jnp.dot(p_l.astype(BF16), v_ref[pl.ds(start, nk), hs], preferred_element_type=F32)
                 + jnp.dot(p_c.astype(BF16), v_ref[0:TM, hs], preferred_element_type=F32))
            o_ref[qrows, hs] = (o / l).astype(BF16)


def _na_attn(qkv, bias, variant):
    _, nb, t, _ = qkv.shape
    rows = (t - TM) // GRID_W
    nrb = rows // NA_QR
    nq = NA_QR * GRID_W
    assert nq == TM
    return pl.pallas_call(
        functools.partial(_na_attn_kernel, rows=rows),
        grid_spec=pltpu.PrefetchScalarGridSpec(
            num_scalar_prefetch=1,
            grid=(NA_H // NA_HPS, nb, nrb // NA_RPS),
            in_specs=[
                pl.BlockSpec((pl.Element(NA_RPS * nq), pl.Element(NA_HPS * NA_DH)),
                             lambda h, b, r, var: (pl.multiple_of(b * t + TM + r * (NA_RPS * nq), TM),
                                                   pl.multiple_of(h * (NA_HPS * NA_DH), NA_HPS * NA_DH))),
                pl.BlockSpec((None, None, t, NA_HPS * NA_DH), lambda h, b, r, var: (1, b, 0, h)),
                pl.BlockSpec((None, None, t, NA_HPS * NA_DH), lambda h, b, r, var: (2, b, 0, h)),
                pl.BlockSpec((NA_HPS, bias.shape[1], nq, NA_KR * GRID_W), lambda h, b, r, var: (h, 0, 0, 0)),
            ],
            out_specs=pl.BlockSpec((None, NA_RPS * nq, NA_HPS * NA_DH), lambda h, b, r, var: (b, r, h)),
        ),
        out_shape=jax.ShapeDtypeStruct((nb, t - TM, D), BF16),
        compiler_params=_cparams(("arbitrary", "arbitrary", "arbitrary")),
        name="na_attn",
    )(variant, qkv.reshape(3 * nb * t, D), qkv, qkv, bias)


def _na_bias_tables(rpb, rows):
    nrb = rows // NA_QR
    rb = np.arange(nrb)
    kb = np.clip(rb * NA_QR - NA_ROWS // 2, 0, rows - NA_KR)
    r_q = rb[:, None] * NA_QR + np.arange(NA_QR)[None, :]
    r0_q = np.clip(r_q - NA_ROWS // 2, 0, rows - NA_ROWS)
    sig = np.concatenate([r_q - kb[:, None], r0_q - kb[:, None]], axis=1)
    uniq, variant = np.unique(sig, axis=0, return_inverse=True)
    variant = np.asarray(variant).reshape(-1)
    kr = np.arange(NA_KR)[None, :]
    cols = np.arange(GRID_W)
    col_start = np.clip(cols - NA_COLS // 2, 0, GRID_W - NA_COLS)
    kc = cols[None, :]
    col_ok = (kc >= col_start[:, None]) & (kc < col_start[:, None] + NA_COLS)
    nh, ndr, ndc = rpb.shape
    per = 2 * GRID_W + 1
    p = jnp.concatenate([rpb[:, :, NA_COLS - 1:], jnp.zeros((nh, ndr, per - ndc), F32), rpb[:, :, :NA_COLS - 1]], axis=2)
    toep = jnp.tile(p, (1, 1, GRID_W))[:, :, :GRID_W * (per - 1)].reshape(nh, ndr, GRID_W, per - 1)[:, :, :, :GRID_W]
    nv = len(uniq)
    sel = np.zeros((nv, NA_QR, NA_KR, ndr), np.float32)
    row_ok = np.zeros((nv, NA_QR, NA_KR), bool)
    for v, u in enumerate(uniq):
        r_abs = u[:NA_QR][:, None]
        r0 = u[NA_QR:][:, None]
        row_ok[v] = (kr >= r0) & (kr < r0 + NA_ROWS)
        drow = np.clip(kr - r_abs + NA_ROWS - 1, 0, ndr - 1)
        sel[v][np.arange(NA_QR)[:, None], np.arange(NA_KR)[None, :], drow] = row_ok[v]
    tab = jnp.einsum("vabd,hdij->hvaibj", jnp.asarray(sel), toep, precision=lax.Precision.HIGHEST)
    ok = row_ok[:, :, None, :, None] & col_ok[None, None, :, None, :]
    tab = jnp.where(ok[None], tab, NEG)
    return tab.reshape(nh, nv, NA_QR * GRID_W, NA_KR * GRID_W), jnp.asarray(variant, I32)


def _swiglu_tile(x_bf16, wg, wu, wd):
    gate = jnp.dot(x_bf16, wg, preferred_element_type=F32)
    up = jnp.dot(x_bf16, wu, preferred_element_type=F32)
    hmid = (_silu(gate) * up).astype(BF16)
    return jnp.dot(hmid, wd, preferred_element_type=F32)


def _dispatch_kernel(pos_ref, padpos_ref, h2_ref, xs_ref, zero_ref, sem, *, nt):
    tile = pl.program_id(0) * nt + pl.program_id(1)

    def pad_copy(q):
        return pltpu.make_async_copy(zero_ref.at[pl.ds(0, 1), :], xs_ref.at[pl.ds(padpos_ref[q], 1), :], sem)

    @pl.when(tile == 0)
    def _():
        zero_ref[...] = jnp.zeros_like(zero_ref)

        def start(q, c):
            pad_copy(q).start()
            return c

        def wait(q, c):
            pad_copy(q).wait()
            return c

        lax.fori_loop(0, N_EXP * TE, start, 0, unroll=16)
        lax.fori_loop(0, N_EXP * TE, wait, 0, unroll=16)

    base = tile * (2 * TM)

    def row_copy(j, kk):
        return pltpu.make_async_copy(h2_ref.at[pl.ds(j, 1), :], xs_ref.at[pl.ds(pos_ref[base + 2 * j + kk], 1), :], sem)

    for j in range(TM):
        row_copy(j, 0).start(priority=0)
        row_copy(j, 1).start(priority=1)
    for j in range(TM):
        row_copy(j, 0).wait()
        row_copy(j, 1).wait()


def _dispatch(h2, pos, padpos, n_rows):
    nb, t, _ = h2.shape
    nt = t // TM
    return pl.pallas_call(
        functools.partial(_dispatch_kernel, nt=nt),
        grid_spec=pltpu.PrefetchScalarGridSpec(
            num_scalar_prefetch=2,
            grid=(nb, nt),
            in_specs=[pl.BlockSpec((TM, D), lambda b, i, *_: (b * nt + i, 0))],
            out_specs=pl.BlockSpec(memory_space=pl.ANY),
            scratch_shapes=[pltpu.VMEM((8, D), F32), pltpu.SemaphoreType.DMA],
        ),
        out_shape=jax.ShapeDtypeStruct((n_rows, D), F32),
        compiler_params=_cparams(("arbitrary", "arbitrary")),
        name="moe_dispatch",
    )(pos, padpos, h2.reshape(nb * t, D))


def _expert_kernel(te_ref, tb_ref, nu_ref, x_ref, wg_ref, wu_ref, wd_ref, y_ref):
    g = pl.program_id(0)

    @pl.when(g < nu_ref[0])
    def _():
        y_ref[...] = _swiglu_tile(x_ref[...].astype(BF16), wg_ref[...], wu_ref[...], wd_ref[...])

    @pl.when(g >= nu_ref[0])
    def _():
        y_ref[...] = jnp.zeros_like(y_ref)


def _experts(xs, tile_expert, tile_block, n_used, wg, wu, wd, layer):
    n_rows = xs.shape[0]
    n_tiles = n_rows // TE
    return pl.pallas_call(
        _expert_kernel,
        grid_spec=pltpu.PrefetchScalarGridSpec(
            num_scalar_prefetch=3,
            grid=(n_tiles,),
            in_specs=[
                pl.BlockSpec((TE, D), lambda g, te, tb, nu: (tb[g], 0)),
                pl.BlockSpec((None, None, D, D_EXP), lambda g, te, tb, nu: (layer, te[g], 0, 0)),
                pl.BlockSpec((None, None, D, D_EXP), lambda g, te, tb, nu: (layer, te[g], 0, 0)),
                pl.BlockSpec((None, None, D_EXP, D), lambda g, te, tb, nu: (layer, te[g], 0, 0)),
            ],
            out_specs=pl.BlockSpec((TE, D), lambda g, te, tb, nu: (g, 0)),
        ),
        out_shape=jax.ShapeDtypeStruct((n_rows, D), F32),
        compiler_params=_cparams(("arbitrary",)),
        name="moe_experts",
    )(tile_expert, tile_block, n_used, xs, wg, wu, wd)


def _combine_kernel(pos_ref, h2_ref, xn_ref, mod_ref, wt_ref, wsg_ref, wsu_ref, wsd_ref, ys_ref, o_ref, ybuf, sem, *, nt):
    tile = pl.program_id(0) * nt + pl.program_id(1)
    base = tile * (2 * TM)

    def row_copy(j, kk):
        return pltpu.make_async_copy(ys_ref.at[pl.ds(pos_ref[base + 2 * j + kk], 1), :], ybuf.at[kk, pl.ds(j, 1), :], sem)

    for j in range(TM):
        row_copy(j, 0).start(priority=0)
        row_copy(j, 1).start(priority=1)
    shared = _swiglu_tile(h2_ref[...].astype(BF16), wsg_ref[...], wsu_ref[...], wsd_ref[...])
    for j in range(TM):
        row_copy(j, 0).wait()
        row_copy(j, 1).wait()
    wt = wt_ref[...]
    y = shared + wt[:, 0:1] * ybuf[0] + wt[:, 1:2] * ybuf[1]
    o_ref[...] = xn_ref[...] + mod_ref[:, 5 * D:6 * D] * y


def _combine(pos, h2, xn, mod, wt, wsg, wsu, wsd, ys, off, layer):
    nb, t, _ = xn.shape
    nt = t // TM
    row = lambda b, i, *_: (b, i, 0)
    const = lambda b, i, *_: (layer, 0, 0)
    return pl.pallas_call(
        functools.partial(_combine_kernel, nt=nt),
        grid_spec=pltpu.PrefetchScalarGridSpec(
            num_scalar_prefetch=1,
            grid=(nb, nt),
            in_specs=[
                pl.BlockSpec((None, TM, D), row),
                pl.BlockSpec((None, TM, D), row),
                pl.BlockSpec((None, 1, ADA * D), lambda b, i, *_: (jnp.where(i + off == 0, nb, b), 0, 0)),
                pl.BlockSpec((None, TM, 8), row),
                _resident((None, D, D_EXP), const),
                _resident((None, D, D_EXP), const),
                _resident((None, D_EXP, D), const),
                pl.BlockSpec(memory_space=pl.ANY),
            ],
            out_specs=pl.BlockSpec((None, TM, D), row),
            scratch_shapes=[pltpu.VMEM((2, TM, D), F32), pltpu.SemaphoreType.DMA],
        ),
        out_shape=jax.ShapeDtypeStruct((nb, t, D), F32),
        compiler_params=_cparams(("arbitrary", "arbitrary")),
        name="moe_combine",
    )(pos, h2, xn, mod, wt, wsg, wsu, wsd, ys)


def _moe_plan(route):
    nb, _, tp = route.shape
    n_tok = nb * tp
    idx = route[:, 0:2, :].astype(I32).transpose(0, 2, 1).reshape(n_tok * 2)
    wt = jnp.pad(route[:, 2:4, :].transpose(0, 2, 1), ((0, 0), (0, 0), (0, 6)))
    onehot = (idx[:, None] == jnp.arange(N_EXP, dtype=I32)[None, :]).astype(I32)
    csum = jnp.cumsum(onehot, axis=0)
    counts = csum[-1]
    rank = jnp.sum((csum - onehot) * onehot, axis=1)
    padded = ((counts + TE - 1) // TE) * TE
    ends = jnp.cumsum(padded)
    starts = ends - padded
    pos = (jnp.sum(onehot * starts[None, :], axis=1) + rank).astype(I32)
    n_tiles = (2 * n_tok) // TE + N_EXP
    n_used = (ends[-1] // TE).astype(I32)
    tile_block = jnp.minimum(jnp.arange(n_tiles, dtype=I32), n_used - 1)
    tile_expert = jnp.sum((tile_block[:, None] * TE >= ends[None, :]).astype(I32), axis=1).astype(I32)
    npads = padded - counts
    pad_ends = jnp.cumsum(npads)
    pad_starts = pad_ends - npads
    qidx = jnp.arange(N_EXP * TE, dtype=I32)
    pe = jnp.minimum(jnp.sum((qidx[:, None] >= pad_ends[None, :]).astype(I32), axis=1), N_EXP - 1)
    in_seg = starts[pe] + counts[pe] + (qidx - pad_starts[pe])
    tail = ends[-1] + (qidx - pad_ends[-1])
    padpos = jnp.where(qidx < pad_ends[-1], in_seg, tail).astype(I32)
    return pos, padpos, tile_expert, tile_block, n_used.reshape(1), wt, n_tiles * TE


def _moe(h2, xn, route, mod, w, off, layer):
    pos, padpos, tile_expert, tile_block, n_used, wt, n_rows = _moe_plan(route)
    xs = _dispatch(h2, pos, padpos, n_rows)
    ys = _experts(xs, tile_expert, tile_block, n_used, w["wg"], w["wu"], w["wd"], layer)
    return _combine(pos, h2, xn, mod, wt, w["wsg"], w["wsu"], w["wsd"], ys, off, layer)


def _rope_tables(t, s):
    half = MLA_ROPE // 2
    inv_freq = ROPE_THETA ** (-jnp.arange(0, half, 2, dtype=F32) / half)
    tt = jnp.arange(t - s)
    ang_row = (tt // GRID_W).astype(F32)[:, None] * inv_freq
    ang_col = (tt % GRID_W).astype(F32)[:, None] * inv_freq
    ang = jnp.concatenate([ang_row] * 2 + [ang_col] * 2, axis=1)
    cos = jnp.cos(ang)
    sin = jnp.sin(ang)
    sign = np.where((np.arange(MLA_ROPE) % 32) < 16, -1.0, 1.0).astype(np.float32)
    sin = sin * sign[None, :]
    cos = jnp.concatenate([jnp.ones((s, MLA_ROPE), F32), cos], axis=0)
    sin = jnp.concatenate([jnp.zeros((s, MLA_ROPE), F32), sin], axis=0)
    cos = jnp.pad(cos, ((0, 0), (0, LANE - MLA_ROPE)), constant_values=1.0)
    sin = jnp.pad(sin, ((0, 0), (0, LANE - MLA_ROPE)))
    return cos.astype(F32), sin.astype(F32)


def _ab_weights(j, t, s, ab_w_in, lru_conv_w, lru_conv_b, lru_w_a, lru_b_a, lru_w_i, lru_b_i, lru_lambda,
                mla_q_norm_g, mla_w_uq, mla_kv_norm_g, mla_w_ukv, mla_q_head_g, mla_k_head_g):
    w_in = jnp.pad(ab_w_in[j], ((0, 0), (0, AB_IN_PAD - AB_IN))).astype(BF16)
    uq = mla_w_uq[j].reshape(Q_RANK, MLA_H, MLA_QK)
    uq_n = uq[:, :, :MLA_NOPE].reshape(Q_RANK, MLA_H * LANE)
    uq_r = jnp.pad(uq[:, :, MLA_NOPE:], ((0, 0), (0, 0), (0, LANE - MLA_ROPE))).reshape(Q_RANK, MLA_H * LANE)
    ukv = mla_w_ukv[j].reshape(KV_RANK, MLA_H, MLA_NOPE + MLA_V)
    ukv_k = ukv[:, :, :MLA_NOPE].reshape(KV_RANK, MLA_H * LANE)
    ukv_v = ukv[:, :, MLA_NOPE:].reshape(KV_RANK, MLA_H * MLA_V)
    scale = MLA_QK ** -0.5 * LOG2E
    qg = mla_q_head_g[j] * scale
    kg = mla_k_head_g[j]
    pad_r = lambda g: jnp.pad(g[MLA_NOPE:], (0, LANE - MLA_ROPE))
    hg = jnp.stack([qg[:MLA_NOPE], pad_r(qg), kg[:MLA_NOPE], pad_r(kg)], axis=0)
    cos, sin = _rope_tables(t, s)
    return dict(
        w_in=w_in, qng=mla_q_norm_g[j][None], kvng=mla_kv_norm_g[j][None],
        w_uq=jnp.concatenate([uq_n, uq_r], axis=1).astype(BF16),
        w_ukv=jnp.concatenate([ukv_k, ukv_v], axis=1).astype(BF16),
        hg=hg, cos=cos, sin=sin,
        conv_w=lru_conv_w[j], conv_b=lru_conv_b[j][None],
        w_a=lru_w_a[j].astype(BF16), b_a=lru_b_a[j][:, None, :],
        w_i=lru_w_i[j].astype(BF16), b_i=lru_b_i[j][:, None, :], lam=lru_lambda[j][:, None, :],
    )


def kernel(x, c, ctx, c_ctx, ada_w, ada_b, norm_mix_g, norm_ffn_g, ab_w_in, ab_w_out, lru_conv_w, lru_conv_b, lru_w_a, lru_b_a, lru_w_i, lru_b_i, lru_lambda, mla_q_norm_g, mla_w_uq, mla_kv_norm_g, mla_w_ukv, mla_q_head_g, mla_k_head_g, na_w_qkv, na_w_out, na_q_head_g, na_k_head_g, na_rpb, router_w, router_b, moe_w_gate, moe_w_up, moe_w_down, moe_ws_gate, moe_ws_up, moe_ws_down):
    nb, seq, _ = x.shape
    s = ctx.shape[1]
    depth = ada_w.shape[0]
    assert s == TM and seq % TM == 0 and nb < 8
    t = s + seq
    rows = seq // GRID_W

    cs = jnp.zeros((8, D), F32).at[:nb].set(c).at[nb].set(c_ctx)
    mod_all = _adaln(cs, ada_w, ada_b).reshape(depth, 8, 1, ADA * D)

    rw_t = router_w.T
    rw_hi = rw_t.astype(BF16)
    rw_lo = jnp.concatenate([rw_hi, (rw_t - rw_hi.astype(F32)).astype(BF16)], axis=0)
    rb = router_b.reshape(N_EXP, 1)

    moe_w = dict(wg=moe_w_gate.astype(BF16), wu=moe_w_up.astype(BF16), wd=moe_w_down.astype(BF16),
                 wsg=moe_ws_gate.astype(BF16), wsu=moe_ws_up.astype(BF16), wsd=moe_ws_down.astype(BF16))

    x_all = (ctx, x)
    for i in range(depth):
        last = i == depth - 1
        off = 1 if last else 0
        j = i // 2
        mod = mod_all[i]
        g_mix = norm_mix_g[i][None]
        g_ffn = norm_ffn_g[i][None]
        if i % 2 == 0:
            w = _ab_weights(j, t, s, ab_w_in, lru_conv_w, lru_conv_b, lru_w_a, lru_b_a, lru_w_i, lru_b_i, lru_lambda,
                            mla_q_norm_g, mla_w_uq, mla_kv_norm_g, mla_w_ukv, mla_q_head_g, mla_k_head_g)
            ux, gate, q, k, v = _ab_in(x_all, nb, t, g_mix, mod, w)
            h_f, h_b = _lru(ux, w)
            att_c, att_l = _mla_attn(q, k, v)
            xn, h2, route = _mixer_out(functools.partial(_ab_out_kernel, off=off), (h_f, h_b, gate, att_c, att_l),
                                       x_all, nb, t, mod, g_ffn, ab_w_out[j].astype(BF16), rw_hi, rw_lo, rb, off)
        else:
            scale = NA_DH ** -0.5 * LOG2E
            hg = jnp.stack([na_q_head_g[j] * scale, na_k_head_g[j], jnp.ones((NA_DH,), F32)], axis=0)[:, None, :]
            qkv = _na_in(x_all, g_mix, mod, na_w_qkv[j].astype(BF16), hg)
            bias, variant = _na_bias_tables(na_rpb[j] * LOG2E, rows)
            att = _na_attn(qkv, bias, variant)
            assert last, "the neighbourhood layer has no context-output path"
            xn, h2, route = _mixer_out(_na_out_kernel, (att,), x_all, nb, t, mod, g_ffn, na_w_out[j].astype(BF16),
                                       rw_hi, rw_lo, rb, off)
        x_all = _moe(h2, xn, route, mod, moe_w, off, i)
    return x_all
```

```python
import functools

import numpy as np
import jax
import jax.numpy as jnp
from jax import lax
from jax.experimental import pallas as pl
from jax.experimental.pallas import tpu as pltpu

F32 = jnp.float32
BF16 = jnp.bfloat16
I32 = jnp.int32

D = 2048
ADA = 6
EPS = 1e-6
NEG = -1e30
LOG2E = 1.4426950408889634
GRID_W = 64
D_RNN = 1024
LRU_BLOCKS = 8
LRU_BW = 128
LRU_C = 8.0
MLA_H = 8
MLA_NOPE = 128
MLA_ROPE = 64
MLA_QK = 192
MLA_V = 128
Q_RANK = 512
KV_RANK = 512
ROPE_THETA = 10000.0
AB_IN = 3136
AB_IN_PAD = 3200
NA_H = 16
NA_DH = 128
NA_ROWS = 8
NA_COLS = 16
N_EXP = 16
N_GROUPS = 4
EXP_PER_GROUP = 4
D_EXP = 1408

LANE = 128
SUBLANES = 8
TM = 256
TE = 256
VMEM_LIMIT = 56 * 1024 * 1024

NA_QR = 4
NA_KR = 12


def _cparams(sem, vmem=VMEM_LIMIT):
    return pltpu.CompilerParams(dimension_semantics=sem, vmem_limit_bytes=vmem)


def _resident(shape, index_map):
    return pl.BlockSpec(shape, index_map, pipeline_mode=pl.Buffered(1))


def _rms(x, g, n):
    ms = jnp.sum(x * x, axis=-1, keepdims=True) * (1.0 / n)
    return (x * lax.rsqrt(ms + EPS)) * g


def _modulate(x, g, shift, scale):
    return _rms(x, g, D) * (1.0 + scale) + shift


def _gelu_tanh(x):
    c = 0.7978845608028654
    return x * (0.5 * (1.0 + jnp.tanh(c * (x + 0.044715 * (x * x * x)))))


def _silu(x):
    return x * jax.nn.sigmoid(x)


ADA_TN = 1536


def _adaln_kernel(cs_ref, w_ref, b_ref, o_ref):
    a = _silu(cs_ref[...]).astype(BF16)
    o_ref[...] = jnp.dot(a, w_ref[...].astype(BF16), preferred_element_type=F32) + b_ref[...]


def _adaln(cs, ada_w, ada_b):
    depth = ada_w.shape[0]
    n = ADA * D
    return pl.pallas_call(
        _adaln_kernel,
        grid=(depth, n // ADA_TN),
        in_specs=[
            pl.BlockSpec((8, D), lambda l, j: (0, 0)),
            pl.BlockSpec((None, D, ADA_TN), lambda l, j: (l, 0, j)),
            pl.BlockSpec((None, 1, ADA_TN), lambda l, j: (l, 0, j)),
        ],
        out_specs=pl.BlockSpec((None, 8, ADA_TN), lambda l, j: (l, 0, j)),
        out_shape=jax.ShapeDtypeStruct((depth, 8, n), F32),
        compiler_params=_cparams(("arbitrary", "arbitrary")),
        name="adaln",
    )(cs, ada_w, ada_b.reshape(depth, 1, n))


def _mod_spec(nb, off):
    return pl.BlockSpec((None, 1, ADA * D), lambda b, i: (jnp.where(i + off == 0, nb, b), 0, 0))


def _split_x(x_in, off):
    ctx_map = lambda b, i: (b, 0, 0)
    if isinstance(x_in, tuple):
        return x_in, ctx_map, lambda b, i: (b, jnp.maximum(i + off - 1, 0), 0)
    return (x_in, x_in), ctx_map, lambda b, i: (b, jnp.maximum(i + off, 1), 0)


def _ab_in_kernel(xc_ref, xl_ref, g_ref, mod_ref, win_ref, qng_ref, kvng_ref, wuq_ref, wukv_ref, hg_ref, cos_ref,
                  sin_ref, ux_ref, gate_ref, q_ref, k_ref, v_ref):
    x = jnp.where(pl.program_id(1) == 0, xc_ref[...], xl_ref[...])
    h = _modulate(x, g_ref[...], mod_ref[:, 0:D], mod_ref[:, D:2 * D]).astype(BF16)
    u = jnp.dot(h, win_ref[...], preferred_element_type=F32)
    ux_ref[...] = u[:, 0:D_RNN].astype(BF16)
    gate_ref[...] = _gelu_tanh(u[:, D_RNN:2 * D_RNN]).astype(BF16)
    o2 = 2 * D_RNN
    cq = _rms(u[:, o2:o2 + Q_RANK], qng_ref[...], Q_RANK).astype(BF16)
    ckv = _rms(u[:, o2 + Q_RANK:o2 + Q_RANK + KV_RANK], kvng_ref[...], KV_RANK).astype(BF16)
    qf = jnp.dot(cq, wuq_ref[...], preferred_element_type=F32)
    kvf = jnp.dot(ckv, wukv_ref[...], preferred_element_type=F32)
    kr = u[:, o2 + Q_RANK + KV_RANK:AB_IN_PAD]
    cos = cos_ref[...]
    sin = sin_ref[...]
    lane = lax.broadcasted_iota(I32, (TM, LANE), 1)
    first_half = (lane % 32) < 16

    def rope(z):
        partner = jnp.where(first_half, pltpu.roll(z, LANE - 16, 1), pltpu.roll(z, 16, 1))
        return z * cos + partner * sin

    hg = hg_ref[...]
    kr_ss = jnp.sum(kr * kr, axis=-1, keepdims=True)
    nh = MLA_H * MLA_NOPE
    for hh in range(MLA_H):
        qn = qf[:, hh * LANE:(hh + 1) * LANE]
        qr = qf[:, nh + hh * LANE:nh + (hh + 1) * LANE]
        ss = jnp.sum(qn * qn, axis=-1, keepdims=True) + jnp.sum(qr * qr, axis=-1, keepdims=True)
        inv = lax.rsqrt(ss * (1.0 / MLA_QK) + EPS)
        q_ref[:, hh * 256:hh * 256 + LANE] = ((qn * inv) * hg[0:1]).astype(BF16)
        q_ref[:, hh * 256 + LANE:(hh + 1) * 256] = rope((qr * inv) * hg[1:2]).astype(BF16)
        kn = kvf[:, hh * LANE:(hh + 1) * LANE]
        ss = jnp.sum(kn * kn, axis=-1, keepdims=True) + kr_ss
        inv = lax.rsqrt(ss * (1.0 / MLA_QK) + EPS)
        k_ref[:, hh * 256:hh * 256 + LANE] = ((kn * inv) * hg[2:3]).astype(BF16)
        k_ref[:, hh * 256 + LANE:(hh + 1) * 256] = rope((kr * inv) * hg[3:4]).astype(BF16)
    v_ref[...] = kvf[:, nh:].astype(BF16)


def _ab_in(x_in, nb, t, g, mod, w):
    nt = t // TM
    (x_c, x_l), ctx_map, lat_map = _split_x(x_in, 0)
    row = lambda b, i: (b, i, 0)
    const = lambda b, i: (0, 0)
    outs = [
        jax.ShapeDtypeStruct((nb, t, D_RNN), BF16),
        jax.ShapeDtypeStruct((nb, t, D_RNN), BF16),
        jax.ShapeDtypeStruct((nb, t, MLA_H * 256), BF16),
        jax.ShapeDtypeStruct((nb, t, MLA_H * 256), BF16),
        jax.ShapeDtypeStruct((nb, t, MLA_H * MLA_V), BF16),
    ]
    return pl.pallas_call(
        _ab_in_kernel,
        grid=(nb, nt),
        in_specs=[
            pl.BlockSpec((None, TM, D), ctx_map),
            pl.BlockSpec((None, TM, D), lat_map),
            _resident((1, D), const),
            _mod_spec(nb, 0),
            _resident((D, AB_IN_PAD), const),
            _resident((1, Q_RANK), const),
            _resident((1, KV_RANK), const),
            _resident((Q_RANK, 2 * MLA_H * LANE), const),
            _resident((KV_RANK, 2 * MLA_H * LANE), const),
            _resident((4, LANE), const),
            pl.BlockSpec((TM, LANE), lambda b, i: (i, 0)),
            pl.BlockSpec((TM, LANE), lambda b, i: (i, 0)),
        ],
        out_specs=[
            pl.BlockSpec((None, TM, D_RNN), row),
            pl.BlockSpec((None, TM, D_RNN), row),
            pl.BlockSpec((None, TM, MLA_H * 256), row),
            pl.BlockSpec((None, TM, MLA_H * 256), row),
            pl.BlockSpec((None, TM, MLA_H * MLA_V), row),
        ],
        out_shape=outs,
        compiler_params=_cparams(("arbitrary", "arbitrary")),
        name="ab_in",
    )(x_c, x_l, g, mod, w["w_in"], w["qng"], w["kvng"], w["w_uq"], w["w_ukv"], w["hg"], w["cos"], w["sin"])


def _lru_dir(prev_ref, cur_ref, next_ref, c, n_chunks, d, reverse,
             cw_ref, cb_ref, wa_ref, ba_ref, wi_ref, bi_ref, lam_ref, carry_ref, out_ref):
    xc = cur_ref[...].astype(F32)
    row = lax.broadcasted_iota(I32, (TM, 1), 0)
    left_ok = jnp.where(c >= 2, 1.0, 0.0)
    right_ok = jnp.where(jnp.logical_and(c >= 1, c <= n_chunks - 2), 1.0, 0.0)
    p0 = prev_ref[TM - 2:TM - 1, :].astype(F32) * left_ok
    p1 = prev_ref[TM - 1:TM, :].astype(F32) * left_ok
    n0 = next_ref[0:1, :].astype(F32) * right_ok
    x_m2 = jnp.where(row >= 2, pltpu.roll(xc, 2, 0), jnp.where(row == 0, p0, p1))
    x_m1 = jnp.where(row >= 1, pltpu.roll(xc, 1, 0), p1)
    x_p1 = jnp.where(row <= TM - 2, pltpu.roll(xc, TM - 1, 0), n0)
    cw = cw_ref[...]
    xconv = x_m2 * cw[0:1] + x_m1 * cw[1:2] + xc * cw[2:3] + x_p1 * cw[3:4] + cb_ref[...]

    xb = xconv.astype(BF16)
    ra = []
    ri = []
    for n in range(LRU_BLOCKS):
        xs = xb[:, n * LRU_BW:(n + 1) * LRU_BW]
        ra.append(jnp.dot(xs, wa_ref[d, n], preferred_element_type=F32))
        ri.append(jnp.dot(xs, wi_ref[d, n], preferred_element_type=F32))
    r = jax.nn.sigmoid(jnp.concatenate(ra, axis=1) + ba_ref[d])
    gi = jax.nn.sigmoid(jnp.concatenate(ri, axis=1) + bi_ref[d])
    neg_lam = -lam_ref[d]
    softplus = jnp.maximum(neg_lam, 0.0) + jnp.log1p(jnp.exp(-jnp.abs(neg_lam)))
    log_a = (-LRU_C * r) * softplus
    a = jnp.exp(log_a)
    th = jnp.tanh(log_a)
    neg_expm1 = (-2.0 * th) / (1.0 - th)
    bb = jnp.sqrt(neg_expm1) * gi * xconv

    ng = TM // SUBLANES
    a3 = a.reshape(ng, SUBLANES, D_RNN)
    b3 = bb.reshape(ng, SUBLANES, D_RNN)
    sub = lax.broadcasted_iota(I32, (1, SUBLANES, 1), 1)
    k = 1
    while k < SUBLANES:
        if reverse:
            keep = sub < SUBLANES - k
            a_s = jnp.where(keep, pltpu.roll(a3, SUBLANES - k, 1), 1.0)
            b_s = jnp.where(keep, pltpu.roll(b3, SUBLANES - k, 1), 0.0)
        else:
            keep = sub >= k
            a_s = jnp.where(keep, pltpu.roll(a3, k, 1), 1.0)
            b_s = jnp.where(keep, pltpu.roll(b3, k, 1), 0.0)
        b3 = b3 + a3 * b_s
        a3 = a3 * a_s
        k *= 2
    hcar = carry_ref[d:d + 1, :]
    last = 0 if reverse else SUBLANES - 1
    for g in (range(ng - 1, -1, -1) if reverse else range(ng)):
        h = b3[g] + a3[g] * hcar
        out_ref[g * SUBLANES:(g + 1) * SUBLANES, :] = h
        hcar = h[last:last + 1, :]
    carry_ref[d:d + 1, :] = hcar


def _lru_kernel(pf_ref, cf_ref, nf_ref, pb_ref, cbk_ref, nb_ref, cw_ref, cb_ref, wa_ref, ba_ref, wi_ref, bi_ref, lam_ref,
                hf_ref, hb_ref, carry_ref, *, n_chunks):
    s = pl.program_id(1)

    @pl.when(s == 0)
    def _():
        carry_ref[...] = jnp.zeros_like(carry_ref)

    params = (cw_ref, cb_ref, wa_ref, ba_ref, wi_ref, bi_ref, lam_ref, carry_ref)
    _lru_dir(pf_ref, cf_ref, nf_ref, s, n_chunks, 0, False, *params, hf_ref)
    cb_idx = jnp.where(s == 0, 0, n_chunks - s)
    _lru_dir(pb_ref, cbk_ref, nb_ref, cb_idx, n_chunks, 1, True, *params, hb_ref)


def _lru(ux, w):
    nb, t, _ = ux.shape
    nt = t // TM
    fwd = lambda s: s
    bwd = lambda s: jnp.where(s == 0, 0, nt - s)
    prev = lambda c: jnp.maximum(c - 1, 0)
    nxt = lambda c: jnp.minimum(c + 1, nt - 1)
    blk = (None, TM, D_RNN)
    const2 = lambda b, s: (0, 0)
    const3 = lambda b, s: (0, 0, 0)
    const4 = lambda b, s: (0, 0, 0, 0)
    return pl.pallas_call(
        functools.partial(_lru_kernel, n_chunks=nt),
        grid=(nb, nt),
        in_specs=[
            pl.BlockSpec(blk, lambda b, s: (b, prev(fwd(s)), 0)),
            pl.BlockSpec(blk, lambda b, s: (b, fwd(s), 0)),
            pl.BlockSpec(blk, lambda b, s: (b, nxt(fwd(s)), 0)),
            pl.BlockSpec(blk, lambda b, s: (b, prev(bwd(s)), 0)),
            pl.BlockSpec(blk, lambda b, s: (b, bwd(s), 0)),
            pl.BlockSpec(blk, lambda b, s: (b, nxt(bwd(s)), 0)),
            _resident((4, D_RNN), const2),
            _resident((1, D_RNN), const2),
            _resident((2, LRU_BLOCKS, LRU_BW, LRU_BW), const4),
            _resident((2, 1, D_RNN), const3),
            _resident((2, LRU_BLOCKS, LRU_BW, LRU_BW), const4),
            _resident((2, 1, D_RNN), const3),
            _resident((2, 1, D_RNN), const3),
        ],
        out_specs=[
            pl.BlockSpec(blk, lambda b, s: (b, fwd(s), 0)),
            pl.BlockSpec(blk, lambda b, s: (b, bwd(s), 0)),
        ],
        out_shape=[jax.ShapeDtypeStruct((nb, t, D_RNN), F32)] * 2,
        scratch_shapes=[pltpu.VMEM((8, D_RNN), F32)],
        compiler_params=_cparams(("arbitrary", "arbitrary")),
        name="lru_scan",
    )(ux, ux, ux, ux, ux, ux, w["conv_w"], w["conv_b"], w["w_a"], w["b_a"], w["w_i"], w["b_i"], w["lam"])


def _softmax_av(q, k, v):
    s = lax.dot_general(q, k, (((1,), (1,)), ((), ())), preferred_element_type=F32)
    m = jnp.max(s, axis=-1, keepdims=True)
    p = jnp.exp2((s - m).astype(BF16))
    l = jnp.sum(p.astype(F32), axis=-1, keepdims=True)
    o = jnp.dot(p, v, preferred_element_type=F32)
    return o / l


MLA_HPS = 4


MLA_TQ = 256


def _mla_attn_kernel(q_ref, k_ref, v_ref, o_ref, *, hps):
    q2d = q_ref.at[0] if len(q_ref.shape) == 3 else q_ref
    for hh in range(hps):
        qs = slice(hh * 256, (hh + 1) * 256)
        vs = slice(hh * MLA_V, (hh + 1) * MLA_V)
        o_ref[:, vs] = _softmax_av(q2d[:, qs], k_ref[:, qs], v_ref[:, vs]).astype(BF16)


def _mla_attn(q, k, v):
    nb, t, _ = q.shape
    lat = t - TM
    hps_c = MLA_H
    ctx_out = pl.pallas_call(
        functools.partial(_mla_attn_kernel, hps=hps_c),
        grid=(nb,),
        in_specs=[
            pl.BlockSpec((None, TM, hps_c * 256), lambda b: (b, 0, 0)),
            pl.BlockSpec((None, TM, hps_c * 256), lambda b: (b, 0, 0)),
            pl.BlockSpec((None, TM, hps_c * MLA_V), lambda b: (b, 0, 0)),
        ],
        out_specs=pl.BlockSpec((None, TM, hps_c * MLA_V), lambda b: (b, 0, 0)),
        out_shape=jax.ShapeDtypeStruct((nb, TM, MLA_H * MLA_V), BF16),
        compiler_params=_cparams(("arbitrary",)),
        name="mla_attn_ctx",
    )(q, k, v)
    hps = MLA_HPS
    lat_out = pl.pallas_call(
        functools.partial(_mla_attn_kernel, hps=hps),
        grid=(nb, MLA_H // hps, lat // MLA_TQ),
        in_specs=[
            pl.BlockSpec((pl.Element(MLA_TQ), pl.Element(hps * 256)),
                         lambda b, h, i: (pl.multiple_of(b * t + TM + i * MLA_TQ, TM),
                                          pl.multiple_of(h * (hps * 256), 256))),
            pl.BlockSpec((None, t, hps * 256), lambda b, h, i: (b, 0, h)),
            pl.BlockSpec((None, t, hps * MLA_V), lambda b, h, i: (b, 0, h)),
        ],
        out_specs=pl.BlockSpec((None, MLA_TQ, hps * MLA_V), lambda b, h, i: (b, i, h)),
        out_shape=jax.ShapeDtypeStruct((nb, lat, MLA_H * MLA_V), BF16),
        compiler_params=_cparams(("arbitrary", "arbitrary", "arbitrary")),
        name="mla_attn",
    )(q.reshape(nb * t, MLA_H * 256), k, v)
    return ctx_out, lat_out


def _route(h2, rwh_ref, rwl_ref, rb_ref, route_ref):
    hi = h2.astype(BF16)
    lo = (h2 - hi.astype(F32)).astype(BF16)
    nt_dims = (((1,), (1,)), ((), ()))
    both = lax.dot_general(rwl_ref[...], hi, nt_dims, preferred_element_type=F32)
    logits = (both[0:N_EXP] + lax.dot_general(rwh_ref[...], lo, nt_dims, preferred_element_type=F32)
              + both[N_EXP:])
    scores = jax.nn.sigmoid(logits)
    sel = scores + rb_ref[...]
    sc = [scores[e:e + 1, :] for e in range(N_EXP)]
    se = [sel[e:e + 1, :] for e in range(N_EXP)]
    gs = []
    for g in range(N_GROUPS):
        a, b, c, d = se[4 * g:4 * g + 4]
        hi_ab, lo_ab = jnp.maximum(a, b), jnp.minimum(a, b)
        hi_cd, lo_cd = jnp.maximum(c, d), jnp.minimum(c, d)
        top1 = jnp.maximum(hi_ab, hi_cd)
        top2 = jnp.maximum(jnp.maximum(lo_ab, lo_cd), jnp.minimum(hi_ab, hi_cd))
        gs.append(top1 + top2)
    best = jnp.zeros_like(gs[0])
    best_v = gs[0]
    for g in range(1, N_GROUPS):
        upd = gs[g] > best_v
        best = jnp.where(upd, float(g), best)
        best_v = jnp.where(upd, gs[g], best_v)
    masked = [jnp.where(best == float(e // EXP_PER_GROUP), se[e], NEG) for e in range(N_EXP)]
    i1 = jnp.zeros_like(best)
    v1 = masked[0]
    s1 = sc[0]
    for e in range(1, N_EXP):
        upd = masked[e] > v1
        i1 = jnp.where(upd, float(e), i1)
        v1 = jnp.where(upd, masked[e], v1)
        s1 = jnp.where(upd, sc[e], s1)
    i2 = jnp.zeros_like(best)
    v2 = jnp.full_like(v1, -jnp.inf)
    s2 = jnp.zeros_like(s1)
    for e in range(N_EXP):
        upd = jnp.logical_and(masked[e] > v2, i1 != float(e))
        i2 = jnp.where(upd, float(e), i2)
        v2 = jnp.where(upd, masked[e], v2)
        s2 = jnp.where(upd, sc[e], s2)
    tot = s1 + s2
    route_ref[0:1, :] = i1
    route_ref[1:2, :] = i2
    route_ref[2:3, :] = s1 / tot
    route_ref[3:4, :] = s2 / tot
    route_ref[4:8, :] = jnp.zeros((4, TM), F32)


def _finish_mixer(y, x, mod_ref, g_ref, rwh_ref, rwl_ref, rb_ref, xn_ref, h2_ref, route_ref):
    xn = x + mod_ref[:, 2 * D:3 * D] * y
    xn_ref[...] = xn
    h2 = _modulate(xn, g_ref[...], mod_ref[:, 3 * D:4 * D], mod_ref[:, 4 * D:5 * D])
    h2_ref[...] = h2
    _route(h2, rwh_ref, rwl_ref, rb_ref, route_ref)


def _ab_out_kernel(hf_ref, hb_ref, gate_ref, attc_ref, attl_ref, xc_ref, xl_ref, mod_ref, g_ref, wo_ref, rwh_ref,
                   rwl_ref, rb_ref, xn_ref, h2_ref, route_ref, *, off):
    rnn = ((hf_ref[...] + hb_ref[...]) * gate_ref[...].astype(F32)).astype(BF16)
    is_ctx = pl.program_id(1) + off == 0
    att = jnp.where(is_ctx, attc_ref[...], attl_ref[...])
    y = (jnp.dot(rnn, wo_ref[0:D_RNN, :], preferred_element_type=F32)
         + jnp.dot(att, wo_ref[D_RNN:, :], preferred_element_type=F32))
    x = jnp.where(is_ctx, xc_ref[...], xl_ref[...])
    _finish_mixer(y, x, mod_ref, g_ref, rwh_ref, rwl_ref, rb_ref, xn_ref, h2_ref, route_ref)


def _na_out_kernel(att_ref, x_ref, mod_ref, g_ref, wo_ref, rwh_ref, rwl_ref, rb_ref, xn_ref, h2_ref, route_ref):
    x2d = x_ref.at[0] if len(x_ref.shape) == 3 else x_ref
    for sub in range(att_ref.shape[0] // TM):
        r = pl.ds(sub * TM, TM)
        y = jnp.dot(att_ref[r, :], wo_ref[...], preferred_element_type=F32)
        _finish_mixer(y, x2d[r, :], mod_ref, g_ref, rwh_ref, rwl_ref, rb_ref,
                      xn_ref.at[r, :], h2_ref.at[r, :], route_ref.at[:, r])


def _mixer_out(kernel, acts, x_in, nb, t, mod, g, wo, rw_hi, rw_lo, rb, off):
    nt = t // TM - off
    sub = 2 if (off == 1 and nt % 2 == 0 and kernel is _na_out_kernel) else 1
    rows = sub * TM
    out_row = lambda b, i: (b, i, 0)
    const = lambda b, i: (0, 0)
    row = lambda b, i: (b, i + off, 0)
    if kernel is _na_out_kernel:
        x_arrays = (x_in,)
        if sub == 1:
            x_specs = [pl.BlockSpec((None, TM, D), row)]
        else:
            x_specs = [pl.BlockSpec((pl.Element(1), pl.Element(rows), pl.Element(D)),
                                    lambda b, i: (b, (sub * i + off) * TM, 0))]
    else:
        x_arrays, ctx_map, lat_map = _split_x(x_in, off)
        x_specs = [pl.BlockSpec((None, TM, D), ctx_map), pl.BlockSpec((None, TM, D), lat_map)]
    def act_map(a):
        if a.shape[1] == t:
            return row
        if a.shape[1] == TM:
            return lambda b, i: (b, 0, 0)
        if off == 0:
            return lambda b, i: (b, jnp.maximum(i - 1, 0), 0)
        return out_row

    act_specs = [pl.BlockSpec((None, rows, a.shape[2]), act_map(a)) for a in acts]
    assert sub == 1 or all(a.shape[1] == t - TM for a in acts)
    return pl.pallas_call(
        kernel,
        grid=(nb, nt // sub),
        in_specs=act_specs + x_specs + [
            _mod_spec(nb, off),
            _resident((1, D), const),
            _resident(wo.shape, const),
            _resident((N_EXP, D), const),
            _resident((2 * N_EXP, D), const),
            _resident((N_EXP, 1), const),
        ],
        out_specs=[
            pl.BlockSpec((None, rows, D), out_row),
            pl.BlockSpec((None, rows, D), out_row),
            pl.BlockSpec((None, 8, rows), lambda b, i: (b, 0, i)),
        ],
        out_shape=[
            jax.ShapeDtypeStruct((nb, nt * TM, D), F32),
            jax.ShapeDtypeStruct((nb, nt * TM, D), F32),
            jax.ShapeDtypeStruct((nb, 8, nt * TM), F32),
        ],
        compiler_params=_cparams(("arbitrary", "arbitrary")),
        name="mixer_out",
    )(*acts, *x_arrays, mod, g, wo, rw_hi, rw_lo, rb)


def _na_in_kernel(x_ref, g_ref, mod_ref, w_ref, hg_ref, o_ref):
    h = _modulate(x_ref[...], g_ref[...], mod_ref[:, 0:D], mod_ref[:, D:2 * D]).astype(BF16)
    for p in range(3):
        u = jnp.dot(h, w_ref[:, p * D:(p + 1) * D], preferred_element_type=F32)
        if p == 2:
            o_ref[p] = u.astype(BF16)
            continue
        hg = hg_ref[p]
        for hh in range(NA_H):
            uh = u[:, hh * NA_DH:(hh + 1) * NA_DH]
            o_ref[p, :, hh * NA_DH:(hh + 1) * NA_DH] = _rms(uh, hg, NA_DH).astype(BF16)


def _na_in(x_all, g, mod, w_qkv, hg):
    nb, t, _ = x_all.shape
    nt = t // TM
    return pl.pallas_call(
        _na_in_kernel,
        grid=(nb, nt),
        in_specs=[
            pl.BlockSpec((None, TM, D), lambda b, i: (b, i, 0)),
            _resident((1, D), lambda b, i: (0, 0)),
            _mod_spec(nb, 0),
            _resident((D, 3 * D), lambda b, i: (0, 0)),
            _resident((3, 1, NA_DH), lambda b, i: (0, 0, 0)),
        ],
        out_specs=pl.BlockSpec((3, None, TM, D), lambda b, i: (0, b, i, 0)),
        out_shape=jax.ShapeDtypeStruct((3, nb, t, D), BF16),
        compiler_params=_cparams(("arbitrary", "arbitrary")),
        name="na_in",
    )(x_all, g, mod, w_qkv, hg)


def _na_key_base(rb, rows):
    return jnp.clip(rb * NA_QR - NA_ROWS // 2, 0, rows - NA_KR)


NA_HPS = 4


NA_RPS = 2


def _na_attn_kernel(var_ref, q_ref, k_ref, v_ref, bias_ref, o_ref, *, rows):
    nk = NA_KR * GRID_W
    nt_dims = (((1,), (1,)), ((), ()))
    for sb in range(NA_RPS):
        rb = pl.program_id(2) * NA_RPS + sb
        start = pl.multiple_of(TM + _na_key_base(rb, rows) * GRID_W, GRID_W)
        var = var_ref[rb]
        qrows = slice(sb * TM, (sb + 1) * TM)
        for hh in range(NA_HPS):
            hs = slice(hh * NA_DH, (hh + 1) * NA_DH)
            q = q_ref[qrows, hs]
            s_c = lax.dot_general(q, k_ref[0:TM, hs], nt_dims, preferred_element_type=F32)
            s_l = (lax.dot_general(q, k_ref[pl.ds(start, nk), hs], nt_dims, preferred_element_type=F32)
                   + bias_ref[hh, var])
            m = jnp.maximum(jnp.max(s_c, axis=-1, keepdims=True), jnp.max(s_l, axis=-1, keepdims=True))
            p_c = jnp.exp2(s_c - m)
            p_l = jnp.exp2(s_l - m)
            l = jnp.sum(p_c, axis=-1, keepdims=True) + jnp.sum(p_l, axis=-1, keepdims=True)
            o = (jnp.dot(p_l.astype(BF16), v_ref[pl.ds(start, nk), hs], preferred_element_type=F32)
                 + jnp.dot(p_c.astype(BF16), v_ref[0:TM, hs], preferred_element_type=F32))
            o_ref[qrows, hs] = (o / l).astype(BF16)


def _na_attn(qkv, bias, variant):
    _, nb, t, _ = qkv.shape
    rows = (t - TM) // GRID_W
    nrb = rows // NA_QR
    nq = NA_QR * GRID_W
    assert nq == TM
    return pl.pallas_call(
        functools.partial(_na_attn_kernel, rows=rows),
        grid_spec=pltpu.PrefetchScalarGridSpec(
            num_scalar_prefetch=1,
            grid=(NA_H // NA_HPS, nb, nrb // NA_RPS),
            in_specs=[
                pl.BlockSpec((pl.Element(NA_RPS * nq), pl.Element(NA_HPS * NA_DH)),
                             lambda h, b, r, var: (pl.multiple_of(b * t + TM + r * (NA_RPS * nq), TM),
                                                   pl.multiple_of(h * (NA_HPS * NA_DH), NA_HPS * NA_DH))),
                pl.BlockSpec((None, None, t, NA_HPS * NA_DH), lambda h, b, r, var: (1, b, 0, h)),
                pl.BlockSpec((None, None, t, NA_HPS * NA_DH), lambda h, b, r, var: (2, b, 0, h)),
                pl.BlockSpec((NA_HPS, bias.shape[1], nq, NA_KR * GRID_W), lambda h, b, r, var: (h, 0, 0, 0)),
            ],
            out_specs=pl.BlockSpec((None, NA_RPS * nq, NA_HPS * NA_DH), lambda h, b, r, var: (b, r, h)),
        ),
        out_shape=jax.ShapeDtypeStruct((nb, t - TM, D), BF16),
        compiler_params=_cparams(("arbitrary", "arbitrary", "arbitrary")),
        name="na_attn",
    )(variant, qkv.reshape(3 * nb * t, D), qkv, qkv, bias)


def _na_bias_tables(rpb, rows):
    nrb = rows // NA_QR
    rb = np.arange(nrb)
    kb = np.clip(rb * NA_QR - NA_ROWS // 2, 0, rows - NA_KR)
    r_q = rb[:, None] * NA_QR + np.arange(NA_QR)[None, :]
    r0_q = np.clip(r_q - NA_ROWS // 2, 0, rows - NA_ROWS)
    sig = np.concatenate([r_q - kb[:, None], r0_q - kb[:, None]], axis=1)
    uniq, variant = np.unique(sig, axis=0, return_inverse=True)
    variant = np.asarray(variant).reshape(-1)
    kr = np.arange(NA_KR)[None, :]
    cols = np.arange(GRID_W)
    col_start = np.clip(cols - NA_COLS // 2, 0, GRID_W - NA_COLS)
    kc = cols[None, :]
    col_ok = (kc >= col_start[:, None]) & (kc < col_start[:, None] + NA_COLS)
    nh, ndr, ndc = rpb.shape
    per = 2 * GRID_W + 1
    p = jnp.concatenate([rpb[:, :, NA_COLS - 1:], jnp.zeros((nh, ndr, per - ndc), F32), rpb[:, :, :NA_COLS - 1]], axis=2)
    toep = jnp.tile(p, (1, 1, GRID_W))[:, :, :GRID_W * (per - 1)].reshape(nh, ndr, GRID_W, per - 1)[:, :, :, :GRID_W]
    nv = len(uniq)
    sel = np.zeros((nv, NA_QR, NA_KR, ndr), np.float32)
    row_ok = np.zeros((nv, NA_QR, NA_KR), bool)
    for v, u in enumerate(uniq):
        r_abs = u[:NA_QR][:, None]
        r0 = u[NA_QR:][:, None]
        row_ok[v] = (kr >= r0) & (kr < r0 + NA_ROWS)
        drow = np.clip(kr - r_abs + NA_ROWS - 1, 0, ndr - 1)
        sel[v][np.arange(NA_QR)[:, None], np.arange(NA_KR)[None, :], drow] = row_ok[v]
    tab = jnp.einsum("vabd,hdij->hvaibj", jnp.asarray(sel), toep, precision=lax.Precision.HIGHEST)
    ok = row_ok[:, :, None, :, None] & col_ok[None, None, :, None, :]
    tab = jnp.where(ok[None], tab, NEG)
    return tab.reshape(nh, nv, NA_QR * GRID_W, NA_KR * GRID_W), jnp.asarray(variant, I32)


def _swiglu_tile(x_bf16, wg, wu, wd):
    gate = jnp.dot(x_bf16, wg, preferred_element_type=F32)
    up = jnp.dot(x_bf16, wu, preferred_element_type=F32)
    hmid = (_silu(gate) * up).astype(BF16)
    return jnp.dot(hmid, wd, preferred_element_type=F32)


def _dispatch_kernel(pos_ref, padpos_ref, h2_ref, xs_ref, zero_ref, sem, *, nt):
    tile = pl.program_id(0) * nt + pl.program_id(1)

    def pad_copy(q):
        return pltpu.make_async_copy(zero_ref.at[pl.ds(0, 1), :], xs_ref.at[pl.ds(padpos_ref[q], 1), :], sem)

    @pl.when(tile == 0)
    def _():
        zero_ref[...] = jnp.zeros_like(zero_ref)

        def start(q, c):
            pad_copy(q).start()
            return c

        def wait(q, c):
            pad_copy(q).wait()
            return c

        lax.fori_loop(0, N_EXP * TE, start, 0, unroll=16)
        lax.fori_loop(0, N_EXP * TE, wait, 0, unroll=16)

    base = tile * (2 * TM)

    def row_copy(j, kk):
        return pltpu.make_async_copy(h2_ref.at[pl.ds(j, 1), :], xs_ref.at[pl.ds(pos_ref[base + 2 * j + kk], 1), :], sem)

    for j in range(TM):
        row_copy(j, 0).start(priority=0)
        row_copy(j, 1).start(priority=1)
    for j in range(TM):
        row_copy(j, 0).wait()
        row_copy(j, 1).wait()


def _dispatch(h2, pos, padpos, n_rows):
    nb, t, _ = h2.shape
    nt = t // TM
    return pl.pallas_call(
        functools.partial(_dispatch_kernel, nt=nt),
        grid_spec=pltpu.PrefetchScalarGridSpec(
            num_scalar_prefetch=2,
            grid=(nb, nt),
            in_specs=[pl.BlockSpec((TM, D), lambda b, i, *_: (b * nt + i, 0))],
            out_specs=pl.BlockSpec(memory_space=pl.ANY),
            scratch_shapes=[pltpu.VMEM((8, D), F32), pltpu.SemaphoreType.DMA],
        ),
        out_shape=jax.ShapeDtypeStruct((n_rows, D), F32),
        compiler_params=_cparams(("arbitrary", "arbitrary")),
        name="moe_dispatch",
    )(pos, padpos, h2.reshape(nb * t, D))


def _expert_kernel(te_ref, tb_ref, nu_ref, x_ref, wg_ref, wu_ref, wd_ref, y_ref):
    g = pl.program_id(0)

    @pl.when(g < nu_ref[0])
    def _():
        y_ref[...] = _swiglu_tile(x_ref[...].astype(BF16), wg_ref[...], wu_ref[...], wd_ref[...])

    @pl.when(g >= nu_ref[0])
    def _():
        y_ref[...] = jnp.zeros_like(y_ref)


def _experts(xs, tile_expert, tile_block, n_used, wg, wu, wd, layer):
    n_rows = xs.shape[0]
    n_tiles = n_rows // TE
    return pl.pallas_call(
        _expert_kernel,
        grid_spec=pltpu.PrefetchScalarGridSpec(
            num_scalar_prefetch=3,
            grid=(n_tiles,),
            in_specs=[
                pl.BlockSpec((TE, D), lambda g, te, tb, nu: (tb[g], 0)),
                pl.BlockSpec((None, None, D, D_EXP), lambda g, te, tb, nu: (layer, te[g], 0, 0)),
                pl.BlockSpec((None, None, D, D_EXP), lambda g, te, tb, nu: (layer, te[g], 0, 0)),
                pl.BlockSpec((None, None, D_EXP, D), lambda g, te, tb, nu: (layer, te[g], 0, 0)),
            ],
            out_specs=pl.BlockSpec((TE, D), lambda g, te, tb, nu: (g, 0)),
        ),
        out_shape=jax.ShapeDtypeStruct((n_rows, D), F32),
        compiler_params=_cparams(("arbitrary",)),
        name="moe_experts",
    )(tile_expert, tile_block, n_used, xs, wg, wu, wd)


def _combine_kernel(pos_ref, h2_ref, xn_ref, mod_ref, wt_ref, wsg_ref, wsu_ref, wsd_ref, ys_ref, o_ref, ybuf, sem, *, nt):
    tile = pl.program_id(0) * nt + pl.program_id(1)
    base = tile * (2 * TM)

    def row_copy(j, kk):
        return pltpu.make_async_copy(ys_ref.at[pl.ds(pos_ref[base + 2 * j + kk], 1), :], ybuf.at[kk, pl.ds(j, 1), :], sem)

    for j in range(TM):
        row_copy(j, 0).start(priority=0)
        row_copy(j, 1).start(priority=1)
    shared = _swiglu_tile(h2_ref[...].astype(BF16), wsg_ref[...], wsu_ref[...], wsd_ref[...])
    for j in range(TM):
        row_copy(j, 0).wait()
        row_copy(j, 1).wait()
    wt = wt_ref[...]
    y = shared + wt[:, 0:1] * ybuf[0] + wt[:, 1:2] * ybuf[1]
    o_ref[...] = xn_ref[...] + mod_ref[:, 5 * D:6 * D] * y


def _combine(pos, h2, xn, mod, wt, wsg, wsu, wsd, ys, off, layer):
    nb, t, _ = xn.shape
    nt = t // TM
    row = lambda b, i, *_: (b, i, 0)
    const = lambda b, i, *_: (layer, 0, 0)
    return pl.pallas_call(
        functools.partial(_combine_kernel, nt=nt),
        grid_spec=pltpu.PrefetchScalarGridSpec(
            num_scalar_prefetch=1,
            grid=(nb, nt),
            in_specs=[
                pl.BlockSpec((None, TM, D), row),
                pl.BlockSpec((None, TM, D), row),
                pl.BlockSpec((None, 1, ADA * D), lambda b, i, *_: (jnp.where(i + off == 0, nb, b), 0, 0)),
                pl.BlockSpec((None, TM, 8), row),
                _resident((None, D, D_EXP), const),
                _resident((None, D, D_EXP), const),
                _resident((None, D_EXP, D), const),
                pl.BlockSpec(memory_space=pl.ANY),
            ],
            out_specs=pl.BlockSpec((None, TM, D), row),
            scratch_shapes=[pltpu.VMEM((2, TM, D), F32), pltpu.SemaphoreType.DMA],
        ),
        out_shape=jax.ShapeDtypeStruct((nb, t, D), F32),
        compiler_params=_cparams(("arbitrary", "arbitrary")),
        name="moe_combine",
    )(pos, h2, xn, mod, wt, wsg, wsu, wsd, ys)


def _moe_plan(route):
    nb, _, tp = route.shape
    n_tok = nb * tp
    idx = route[:, 0:2, :].astype(I32).transpose(0, 2, 1).reshape(n_tok * 2)
    wt = jnp.pad(route[:, 2:4, :].transpose(0, 2, 1), ((0, 0), (0, 0), (0, 6)))
    onehot = (idx[:, None] == jnp.arange(N_EXP, dtype=I32)[None, :]).astype(I32)
    csum = jnp.cumsum(onehot, axis=0)
    counts = csum[-1]
    rank = jnp.sum((csum - onehot) * onehot, axis=1)
    padded = ((counts + TE - 1) // TE) * TE
    ends = jnp.cumsum(padded)
    starts = ends - padded
    pos = (jnp.sum(onehot * starts[None, :], axis=1) + rank).astype(I32)
    n_tiles = (2 * n_tok) // TE + N_EXP
    n_used = (ends[-1] // TE).astype(I32)
    tile_block = jnp.minimum(jnp.arange(n_tiles, dtype=I32), n_used - 1)
    tile_expert = jnp.sum((tile_block[:, None] * TE >= ends[None, :]).astype(I32), axis=1).astype(I32)
    npads = padded - counts
    pad_ends = jnp.cumsum(npads)
    pad_starts = pad_ends - npads
    qidx = jnp.arange(N_EXP * TE, dtype=I32)
    pe = jnp.minimum(jnp.sum((qidx[:, None] >= pad_ends[None, :]).astype(I32), axis=1), N_EXP - 1)
    in_seg = starts[pe] + counts[pe] + (qidx - pad_starts[pe])
    tail = ends[-1] + (qidx - pad_ends[-1])
    padpos = jnp.where(qidx < pad_ends[-1], in_seg, tail).astype(I32)
    return pos, padpos, tile_expert, tile_block, n_used.reshape(1), wt, n_tiles * TE


def _moe(h2, xn, route, mod, w, off, layer):
    pos, padpos, tile_expert, tile_block, n_used, wt, n_rows = _moe_plan(route)
    xs = _dispatch(h2, pos, padpos, n_rows)
    ys = _experts(xs, tile_expert, tile_block, n_used, w["wg"], w["wu"], w["wd"], layer)
    return _combine(pos, h2, xn, mod, wt, w["wsg"], w["wsu"], w["wsd"], ys, off, layer)


def _rope_tables(t, s):
    half = MLA_ROPE // 2
    inv_freq = ROPE_THETA ** (-jnp.arange(0, half, 2, dtype=F32) / half)
    tt = jnp.arange(t - s)
    ang_row = (tt // GRID_W).astype(F32)[:, None] * inv_freq
    ang_col = (tt % GRID_W).astype(F32)[:, None] * inv_freq
    ang = jnp.concatenate([ang_row] * 2 + [ang_col] * 2, axis=1)
    cos = jnp.cos(ang)
    sin = jnp.sin(ang)
    sign = np.where((np.arange(MLA_ROPE) % 32) < 16, -1.0, 1.0).astype(np.float32)
    sin = sin * sign[None, :]
    cos = jnp.concatenate([jnp.ones((s, MLA_ROPE), F32), cos], axis=0)
    sin = jnp.concatenate([jnp.zeros((s, MLA_ROPE), F32), sin], axis=0)
    cos = jnp.pad(cos, ((0, 0), (0, LANE - MLA_ROPE)), constant_values=1.0)
    sin = jnp.pad(sin, ((0, 0), (0, LANE - MLA_ROPE)))
    return cos.astype(F32), sin.astype(F32)


def _ab_weights(j, t, s, ab_w_in, lru_conv_w, lru_conv_b, lru_w_a, lru_b_a, lru_w_i, lru_b_i, lru_lambda,
                mla_q_norm_g, mla_w_uq, mla_kv_norm_g, mla_w_ukv, mla_q_head_g, mla_k_head_g):
    w_in = jnp.pad(ab_w_in[j], ((0, 0), (0, AB_IN_PAD - AB_IN))).astype(BF16)
    uq = mla_w_uq[j].reshape(Q_RANK, MLA_H, MLA_QK)
    uq_n = uq[:, :, :MLA_NOPE].reshape(Q_RANK, MLA_H * LANE)
    uq_r = jnp.pad(uq[:, :, MLA_NOPE:], ((0, 0), (0, 0), (0, LANE - MLA_ROPE))).reshape(Q_RANK, MLA_H * LANE)
    ukv = mla_w_ukv[j].reshape(KV_RANK, MLA_H, MLA_NOPE + MLA_V)
    ukv_k = ukv[:, :, :MLA_NOPE].reshape(KV_RANK, MLA_H * LANE)
    ukv_v = ukv[:, :, MLA_NOPE:].reshape(KV_RANK, MLA_H * MLA_V)
    scale = MLA_QK ** -0.5 * LOG2E
    qg = mla_q_head_g[j] * scale
    kg = mla_k_head_g[j]
    pad_r = lambda g: jnp.pad(g[MLA_NOPE:], (0, LANE - MLA_ROPE))
    hg = jnp.stack([qg[:MLA_NOPE], pad_r(qg), kg[:MLA_NOPE], pad_r(kg)], axis=0)
    cos, sin = _rope_tables(t, s)
    return dict(
        w_in=w_in, qng=mla_q_norm_g[j][None], kvng=mla_kv_norm_g[j][None],
        w_uq=jnp.concatenate([uq_n, uq_r], axis=1).astype(BF16),
        w_ukv=jnp.concatenate([ukv_k, ukv_v], axis=1).astype(BF16),
        hg=hg, cos=cos, sin=sin,
        conv_w=lru_conv_w[j], conv_b=lru_conv_b[j][None],
        w_a=lru_w_a[j].astype(BF16), b_a=lru_b_a[j][:, None, :],
        w_i=lru_w_i[j].astype(BF16), b_i=lru_b_i[j][:, None, :], lam=lru_lambda[j][:, None, :],
    )


def kernel(x, c, ctx, c_ctx, ada_w, ada_b, norm_mix_g, norm_ffn_g, ab_w_in, ab_w_out, lru_conv_w, lru_conv_b, lru_w_a, lru_b_a, lru_w_i, lru_b_i, lru_lambda, mla_q_norm_g, mla_w_uq, mla_kv_norm_g, mla_w_ukv, mla_q_head_g, mla_k_head_g, na_w_qkv, na_w_out, na_q_head_g, na_k_head_g, na_rpb, router_w, router_b, moe_w_gate, moe_w_up, moe_w_down, moe_ws_gate, moe_ws_up, moe_ws_down):
    nb, seq, _ = x.shape
    s = ctx.shape[1]
    depth = ada_w.shape[0]
    assert s == TM and seq % TM == 0 and nb < 8
    t = s + seq
    rows = seq // GRID_W

    cs = jnp.zeros((8, D), F32).at[:nb].set(c).at[nb].set(c_ctx)
    mod_all = _adaln(cs, ada_w, ada_b).reshape(depth, 8, 1, ADA * D)

    rw_t = router_w.T
    rw_hi = rw_t.astype(BF16)
    rw_lo = jnp.concatenate([rw_hi, (rw_t - rw_hi.astype(F32)).astype(BF16)], axis=0)
    rb = router_b.reshape(N_EXP, 1)

    moe_w = dict(wg=moe_w_gate.astype(BF16), wu=moe_w_up.astype(BF16), wd=moe_w_down.astype(BF16),
                 wsg=moe_ws_gate.astype(BF16), wsu=moe_ws_up.astype(BF16), wsd=moe_ws_down.astype(BF16))

    x_all = (ctx, x)
    for i in range(depth):
        last = i == depth - 1
        off = 1 if last else 0
        j = i // 2
        mod = mod_all[i]
        g_mix = norm_mix_g[i][None]
        g_ffn = norm_ffn_g[i][None]
        if i % 2 == 0:
            w = _ab_weights(j, t, s, ab_w_in, lru_conv_w, lru_conv_b, lru_w_a, lru_b_a, lru_w_i, lru_b_i, lru_lambda,
                            mla_q_norm_g, mla_w_uq, mla_kv_norm_g, mla_w_ukv, mla_q_head_g, mla_k_head_g)
            ux, gate, q, k, v = _ab_in(x_all, nb, t, g_mix, mod, w)
            h_f, h_b = _lru(ux, w)
            att_c, att_l = _mla_attn(q, k, v)
            xn, h2, route = _mixer_out(functools.partial(_ab_out_kernel, off=off), (h_f, h_b, gate, att_c, att_l),
                                       x_all, nb, t, mod, g_ffn, ab_w_out[j].astype(BF16), rw_hi, rw_lo, rb, off)
        else:
            scale = NA_DH ** -0.5 * LOG2E
            hg = jnp.stack([na_q_head_g[j] * scale, na_k_head_g[j], jnp.ones((NA_DH,), F32)], axis=0)[:, None, :]
            qkv = _na_in(x_all, g_mix, mod, na_w_qkv[j].astype(BF16), hg)
            bias, variant = _na_bias_tables(na_rpb[j] * LOG2E, rows)
            att = _na_attn(qkv, bias, variant)
            assert last, "the neighbourhood layer has no context-output path"
            xn, h2, route = _mixer_out(_na_out_kernel, (att,), x_all, nb, t, mod, g_ffn, na_w_out[j].astype(BF16),
                                       rw_hi, rw_lo, rb, off)
        x_all = _moe(h2, xn, route, mod, moe_w, off, i)
    return x_all
```
